```python
import math, functools
import jax, jax.numpy as jnp
from jax import lax
import numpy as np

D_MODEL = 2048
BATCH = 1
SEQ = 8192
DEPTH = 2
DEC_BATCH = 128
DEC_SEQ = 8
PAST_LEN = 16384
PAGE_SIZE = 128

N_BRANCH = 4
BRANCH_W = D_MODEL // 4
H_A = 4
D_NOPE = 128
D_ROPE = 64
DV_A = BRANCH_W // H_A
Q_LORA = 384
KV_LORA = 256
MLA_SCALE = (D_NOPE + D_ROPE) ** -0.5
H_B = 4
G_B = 1
DK_B = 64
DV_B = BRANCH_W // H_B
DIFF_SCALE = DK_B ** -0.5
H_C = 4
DK_C = 128
DV_C = BRANCH_W // H_C
H_D = 4
DK_D = 64
DV_D = BRANCH_W // H_D
D_FF = -(-8 * D_MODEL // (3 * 256)) * 256
N_BUCKETS = 32
T5_MAX_EXACT = 16
T5_MAX_DIST = 128
ROPE_THETA = 10000.0
CHUNK = 64
Q_BLOCK = 128
EPS = 1e-6
D_IN = (Q_LORA + KV_LORA + D_ROPE + H_B * 2 * DK_B + G_B * 2 * DK_B + G_B * DV_B
        + 2 * H_C * DK_C + 2 * H_C * DV_C + 2 * H_D * DK_D + 2 * H_D * DV_D + N_BRANCH * D_MODEL)

kernel_name = 'hybrid_mla_diff_hgrn2_retention_step'


def in_splits():
    return [Q_LORA, KV_LORA, D_ROPE,
            H_B * 2 * DK_B, G_B * 2 * DK_B, G_B * DV_B,
            H_C * DK_C, H_C * DK_C, H_C * DV_C, H_C * DV_C,
            H_D * DK_D, H_D * DK_D, H_D * DV_D, H_D * DV_D,
            N_BRANCH * D_MODEL]


def rms_norm(x, g):
    x32 = x.astype(jnp.float32)
    y = x32 * lax.rsqrt(jnp.mean(x32 * x32, axis=-1, keepdims=True) + EPS)
    return (y * g.astype(jnp.float32)).astype(x.dtype)


def rope(x, pos):
    half = x.shape[-1] // 2
    inv = ROPE_THETA ** (-jnp.arange(half, dtype=jnp.float32) / half)
    ang = pos.astype(jnp.float32)[:, None] * inv[None, :]
    ang = ang.reshape((pos.shape[0],) + (1,) * (x.ndim - 3) + (half,))
    cos, sin = jnp.cos(ang), jnp.sin(ang)
    x32 = x.astype(jnp.float32)
    x1, x2 = x32[..., :half], x32[..., half:]
    return jnp.concatenate([x1 * cos - x2 * sin, x1 * sin + x2 * cos], axis=-1).astype(x.dtype)


def t5_bucket(rel):
    n = jnp.maximum(rel, 0)
    large = T5_MAX_EXACT + (jnp.log(jnp.maximum(n, 1).astype(jnp.float32) / T5_MAX_EXACT)
                            / math.log(T5_MAX_DIST / T5_MAX_EXACT)
                            * (N_BUCKETS - T5_MAX_EXACT)).astype(jnp.int32)
    return jnp.where(n < T5_MAX_EXACT, n, jnp.minimum(large, N_BUCKETS - 1))


def rel_bias(table, q_pos, k_pos):
    b = t5_bucket(q_pos[:, None] - k_pos[None, :])
    bias = jnp.transpose(table[b], (2, 0, 1))
    return bias.reshape(G_B, H_B // G_B, q_pos.shape[0], k_pos.shape[0]).astype(jnp.float32)


def to_blocks(a):
    B, T = a.shape[:2]
    return jnp.swapaxes(a.reshape((B, T // Q_BLOCK, Q_BLOCK) + a.shape[2:]), 0, 1)


def from_blocks(o):
    n, B, qb = o.shape[:3]
    return jnp.swapaxes(o, 0, 1).reshape((B, n * qb) + o.shape[3:])


def mla_core(q_lat, q_rope, c_kv, k_rope, mask):
    s = (jnp.einsum('bqhl,bkl->bhqk', q_lat, c_kv)
         + jnp.einsum('bqhr,bkr->bhqk', q_rope, k_rope)).astype(jnp.float32) * MLA_SCALE
    p = jax.nn.softmax(jnp.where(mask, s, -jnp.inf), axis=-1)
    return jnp.einsum('bhqk,bkl->bqhl', p.astype(c_kv.dtype), c_kv)


def mla_attn_prompt(q_lat, q_rope, c_kv, k_rope, pos):
    def blk(xs):
        ql, qr, qp = xs
        return mla_core(ql, qr, c_kv, k_rope, pos[None, :] <= qp[:, None])
    return from_blocks(lax.map(blk, (to_blocks(q_lat), to_blocks(q_rope), pos.reshape(-1, Q_BLOCK))))


def mla_attn_sample(q_lat, q_rope, c_kv, k_rope, pos, li, cache_lat, cache_rope, page_table):
    k_pos = jnp.arange(PAST_LEN + pos.shape[0], dtype=jnp.int32)
    mask = k_pos[None, :] <= pos[:, None]
    def seq(xs):
        pt, ql, qr, cn, rn = xs
        c = jnp.concatenate([cache_lat[li, pt].reshape(-1, KV_LORA), cn], axis=0)
        r = jnp.concatenate([cache_rope[li, pt].reshape(-1, D_ROPE), rn], axis=0)
        return mla_core(ql[None], qr[None], c[None], r[None], mask)[0]
    return lax.map(seq, (page_table, q_lat, q_rope, c_kv, k_rope))


def diff_core(q, k, v, bias, mask, lam):
    s = jnp.einsum('bqgrmd,bkgmd->bmgrqk', q, k).astype(jnp.float32) * DIFF_SCALE + bias
    p = jax.nn.softmax(jnp.where(mask, s, -jnp.inf), axis=-1)
    a = p[:, 0] - lam * p[:, 1]
    return jnp.einsum('bgrqk,bkgv->bqgrv', a.astype(v.dtype), v)


def diff_attn_prompt(q, k, v, lam, table, pos):
    def blk(xs):
        qb, qp = xs
        return diff_core(qb, k, v, rel_bias(table, qp, pos), pos[None, :] <= qp[:, None], lam)
    return from_blocks(lax.map(blk, (to_blocks(q), pos.reshape(-1, Q_BLOCK))))


def diff_attn_sample(q, k, v, lam, table, pos, li, cache_k, cache_v, page_table):
    k_pos = jnp.arange(PAST_LEN + pos.shape[0], dtype=jnp.int32)
    mask = k_pos[None, :] <= pos[:, None]
    bias = rel_bias(table, pos, k_pos)
    def seq(xs):
        pt, qs, ks, vs = xs
        kk = jnp.concatenate([cache_k[li, pt].reshape(-1, G_B, 2, DK_B), ks], axis=0)
        vv = jnp.concatenate([cache_v[li, pt].reshape(-1, G_B, DV_B), vs], axis=0)
        return diff_core(qs[None], kk[None], vv[None], bias, mask, lam)[0]
    return lax.map(seq, (page_table, q, k, v))


def chunk_recurrence(q, k, v, log_f, s0):
    B, T, H, dk = q.shape
    dv = v.shape[-1]
    c = CHUNK if T % CHUNK == 0 else T
    n = T // c
    def blocks(a):
        return jnp.transpose(a.reshape(B, n, c, H, a.shape[-1]), (1, 0, 3, 2, 4))
    mask = jnp.tril(jnp.ones((c, c), dtype=bool))[:, :, None]
    def step(s, xs):
        qc, kc, vc, gc = (a.astype(jnp.float32) for a in xs)
        s32 = s.astype(jnp.float32)
        b = jnp.cumsum(gc, axis=2)
        rel = b[:, :, :, None, :] - b[:, :, None, :, :]
        decay = jnp.exp(jnp.where(mask, rel, -jnp.inf))
        scores = jnp.einsum('bhtc,bhsc,bhtsc->bhts', qc, kc, decay)
        o = (jnp.einsum('bhts,bhsv->bhtv', scores, vc)
             + jnp.einsum('bhtc,bhcv->bhtv', qc * jnp.exp(b), s32))
        b_last = b[:, :, -1, :]
        s_new = (jnp.exp(b_last)[..., None] * s32
                 + jnp.einsum('bhsc,bhsv->bhcv', kc * jnp.exp(b_last[:, :, None, :] - b), vc))
        return s_new.astype(s.dtype), o.astype(v.dtype)
    s_fin, o = lax.scan(step, s0, (blocks(q), blocks(k), blocks(v), blocks(log_f)))
    o = jnp.transpose(o, (1, 0, 3, 2, 4)).reshape(B, T, H, dv)
    return o, s_fin


def trunk_layer(x, pos, p, mla_attn, diff_attn, s_hgrn0, s_ret0, t5_table, lam_init, hgrn_lb):
    B, T, _ = x.shape
    f32 = jnp.float32
    h = rms_norm(x, p['norm_attn'])
    z = h @ p['w_in']
    offs = np.cumsum(np.array(in_splits()))[:-1].tolist()
    (qa, kva, kra, bq, bk, bv, cq, cf, ci, cg, rq, rk, rv, rg, gl) = jnp.split(z, offs, axis=-1)

    q_a = jnp.einsum('btr,rhd->bthd', rms_norm(qa, p['mla_q_norm']), p['mla_w_qb'])
    q_a = rms_norm(q_a, p['mla_q_head_norm'])
    q_lat = jnp.einsum('bthn,lhn->bthl', q_a[..., :D_NOPE], p['mla_w_uk'])
    q_rope = rope(q_a[..., D_NOPE:], pos)
    c_kv = rms_norm(kva, p['mla_kv_norm'])
    k_rope = rope(rms_norm(kra, p['mla_krope_norm']), pos)
    o_lat = mla_attn(q_lat, q_rope, c_kv, k_rope, pos)
    o_a = jnp.einsum('bthl,lhv->bthv', o_lat, p['mla_w_uv']).reshape(B, T, BRANCH_W)

    q_b = rms_norm(bq.reshape(B, T, G_B, H_B // G_B, 2, DK_B), p['diff_q_norm'])
    k_b = rms_norm(bk.reshape(B, T, G_B, 2, DK_B), p['diff_k_norm'])
    v_b = bv.reshape(B, T, G_B, DV_B)
    lv = p['diff_lambda'].astype(f32)
    lam = jnp.exp(jnp.sum(lv[0] * lv[1])) - jnp.exp(jnp.sum(lv[2] * lv[3])) + lam_init
    o_b = diff_attn(q_b, k_b, v_b, lam, t5_table, pos)
    o_b = (rms_norm(o_b, p['diff_out_norm']) * (1.0 - lam_init)).reshape(B, T, BRANCH_W)

    lb = hgrn_lb.reshape(H_C, DK_C)
    f_raw = cf.reshape(B, T, H_C, DK_C).astype(f32)
    log_f = jnp.logaddexp(jnp.log(lb), jnp.log1p(-lb) + jax.nn.log_sigmoid(f_raw))
    k_c = ((1.0 - lb) * jax.nn.sigmoid(-f_raw)).astype(x.dtype)
    q_c = jax.nn.silu(cq.reshape(B, T, H_C, DK_C))
    o_c, s_hgrn = chunk_recurrence(q_c, k_c, ci.reshape(B, T, H_C, DV_C), log_f, s_hgrn0)
    o_c = (rms_norm(o_c, p['hgrn_out_norm'])
           * jax.nn.sigmoid(cg.reshape(B, T, H_C, DV_C))).reshape(B, T, BRANCH_W)

    q_d = rope(rq.reshape(B, T, H_D, DK_D), pos)
    k_d = rope(rk.reshape(B, T, H_D, DK_D), pos) * (DK_D ** -0.5)
    log_g = jnp.log1p(-jnp.exp2(-5.0 - jnp.arange(H_D, dtype=f32)))
    log_g = jnp.broadcast_to(log_g[None, None, :, None], (B, T, H_D, DK_D))
    o_d, s_ret = chunk_recurrence(q_d, k_d, rv.reshape(B, T, H_D, DV_D), log_g, s_ret0)
    o_d = (rms_norm(o_d, p['ret_out_norm'])
           * jax.nn.silu(rg.reshape(B, T, H_D, DV_D))).reshape(B, T, BRANCH_W)

    branches = jnp.stack([o_a, o_b, o_c, o_d], axis=2)
    proj = jnp.einsum('btnw,nwd->btnd', branches, p['w_branch'])
    gate = jax.nn.sigmoid(gl.reshape(B, T, N_BRANCH, D_MODEL))
    x = x + jnp.sum(gate * proj, axis=2) @ p['w_out']

    h = rms_norm(x, p['norm_ffn'])
    x = x + (jax.nn.silu(h @ p['w_gate']) * (h @ p['w_up'])) @ p['w_down']
    return x, (c_kv, k_rope, k_b.reshape(B, T, G_B, 2 * DK_B), v_b, s_hgrn, s_ret)


def setup_inputs(seed: int = 0) -> dict:
    key = jax.random.key(seed)
    k = jax.random.split(key, 32)
    f32 = jnp.float32
    def nrm(i, shape, scale):
        return jax.random.normal(k[i], shape, f32) * scale
    def gain(i, shape):
        return 1.0 + 0.01 * jax.random.normal(k[i], shape, f32)
    n_pages = PAST_LEN // PAGE_SIZE
    n_used = DEC_BATCH * n_pages
    n_pool = n_used + n_used // 4
    page_table = jax.random.permutation(k[8], n_pool)[:n_used].reshape(DEC_BATCH, n_pages).astype(jnp.int32)
    return {
        'x_prompt': nrm(0, (BATCH, SEQ, D_MODEL), 1.0),
        'x_sample': nrm(1, (DEC_BATCH, DEC_SEQ, D_MODEL), 1.0),
        'cache_mla_latent': nrm(2, (DEPTH, n_pool, PAGE_SIZE, KV_LORA), 1.0),
        'cache_mla_rope': nrm(3, (DEPTH, n_pool, PAGE_SIZE, D_ROPE), 1.0),
        'cache_diff_k': nrm(4, (DEPTH, n_pool, PAGE_SIZE, G_B, 2 * DK_B), 1.0),
        'cache_diff_v': nrm(5, (DEPTH, n_pool, PAGE_SIZE, G_B, DV_B), 1.0),
        'state_hgrn': nrm(6, (DEPTH, DEC_BATCH, H_C, DK_C, DV_C), 0.5),
        'state_ret': nrm(7, (DEPTH, DEC_BATCH, H_D, DK_D, DV_D), 0.5),
        'page_table': page_table,
        't5_bias': nrm(9, (N_BUCKETS, H_B), 0.5),
        'norm_attn': gain(10, (DEPTH, D_MODEL)),
        'w_in': nrm(11, (DEPTH, D_MODEL, D_IN), D_MODEL ** -0.5),
        'mla_q_norm': gain(12, (DEPTH, Q_LORA)),
        'mla_w_qb': nrm(13, (DEPTH, Q_LORA, H_A, D_NOPE + D_ROPE), Q_LORA ** -0.5),
        'mla_q_head_norm': gain(14, (DEPTH, D_NOPE + D_ROPE)),
        'mla_kv_norm': gain(15, (DEPTH, KV_LORA)),
        'mla_krope_norm': gain(16, (DEPTH, D_ROPE)),
        'mla_w_uk': nrm(17, (DEPTH, KV_LORA, H_A, D_NOPE), KV_LORA ** -0.5),
        'mla_w_uv': nrm(18, (DEPTH, KV_LORA, H_A, DV_A), KV_LORA ** -0.5),
        'diff_q_norm': gain(19, (DEPTH, DK_B)),
        'diff_k_norm': gain(20, (DEPTH, DK_B)),
        'diff_lambda': nrm(21, (DEPTH, 4, DK_B), 0.1),
        'diff_out_norm': gain(22, (DEPTH, DV_B)),
        'hgrn_lb_logits': nrm(23, (DEPTH, H_C * DK_C), 1.0),
        'hgrn_out_norm': gain(24, (DEPTH, DV_C)),
        'ret_out_norm': gain(25, (DEPTH, DV_D)),
        'w_branch': nrm(26, (DEPTH, N_BRANCH, BRANCH_W, D_MODEL), BRANCH_W ** -0.5),
        'w_out': nrm(27, (DEPTH, D_MODEL, D_MODEL), D_MODEL ** -0.5),
        'norm_ffn': gain(28, (DEPTH, D_MODEL)),
        'w_gate': nrm(29, (DEPTH, D_MODEL, D_FF), D_MODEL ** -0.5),
        'w_up': nrm(30, (DEPTH, D_MODEL, D_FF), D_MODEL ** -0.5),
        'w_down': nrm(31, (DEPTH, D_FF, D_MODEL), D_FF ** -0.5),
    }


def reference(x_prompt, x_sample, cache_mla_latent, cache_mla_rope, cache_diff_k, cache_diff_v,
              state_hgrn, state_ret, page_table, t5_bias, norm_attn, w_in, mla_q_norm, mla_w_qb,
              mla_q_head_norm, mla_kv_norm, mla_krope_norm, mla_w_uk, mla_w_uv, diff_q_norm,
              diff_k_norm, diff_lambda, diff_out_norm, hgrn_lb_logits, hgrn_out_norm, ret_out_norm,
              w_branch, w_out, norm_ffn, w_gate, w_up, w_down):
    f32 = jnp.float32
    pos_p = jnp.arange(x_prompt.shape[1], dtype=jnp.int32)
    pos_s = PAST_LEN + jnp.arange(x_sample.shape[1], dtype=jnp.int32)
    lb_cum = jnp.cumsum(jax.nn.softmax(hgrn_lb_logits.astype(f32), axis=0), axis=0)
    lb_all = lb_cum - lb_cum[:1]
    y_p, y_s = x_prompt, x_sample
    new_p, new_s = [], []
    for li in range(DEPTH):
        p = {'norm_attn': norm_attn[li], 'w_in': w_in[li], 'mla_q_norm': mla_q_norm[li],
             'mla_w_qb': mla_w_qb[li], 'mla_q_head_norm': mla_q_head_norm[li],
             'mla_kv_norm': mla_kv_norm[li], 'mla_krope_norm': mla_krope_norm[li],
             'mla_w_uk': mla_w_uk[li], 'mla_w_uv': mla_w_uv[li], 'diff_q_norm': diff_q_norm[li],
             'diff_k_norm': diff_k_norm[li], 'diff_lambda': diff_lambda[li],
             'diff_out_norm': diff_out_norm[li], 'hgrn_out_norm': hgrn_out_norm[li],
             'ret_out_norm': ret_out_norm[li], 'w_branch': w_branch[li], 'w_out': w_out[li],
             'norm_ffn': norm_ffn[li], 'w_gate': w_gate[li], 'w_up': w_up[li], 'w_down': w_down[li]}
        lam_init = 0.8 - 0.6 * math.exp(-0.3 * li)
        zero_c = jnp.zeros((x_prompt.shape[0], H_C, DK_C, DV_C), x_prompt.dtype)
        zero_d = jnp.zeros((x_prompt.shape[0], H_D, DK_D, DV_D), x_prompt.dtype)
        y_p, st_p = trunk_layer(y_p, pos_p, p, mla_attn_prompt, diff_attn_prompt,
                                zero_c, zero_d, t5_bias, lam_init, lb_all[li])
        mla_s = functools.partial(mla_attn_sample, li=li, cache_lat=cache_mla_latent,
                                  cache_rope=cache_mla_rope, page_table=page_table)
        diff_s = functools.partial(diff_attn_sample, li=li, cache_k=cache_diff_k,
                                   cache_v=cache_diff_v, page_table=page_table)
        y_s, st_s = trunk_layer(y_s, pos_s, p, mla_s, diff_s, state_hgrn[li], state_ret[li],
                                t5_bias, lam_init, lb_all[li])
        new_p.append(st_p)
        new_s.append(st_s)
    lat_p, rope_p, dk_p, dv_p, hgrn_p, ret_p = [jnp.stack(a) for a in zip(*new_p)]
    lat_s, rope_s, dk_s, dv_s, hgrn_s, ret_s = [jnp.stack(a) for a in zip(*new_s)]
    return (y_p, y_s, lat_p, rope_p, dk_p, dv_p, hgrn_p, ret_p,
            lat_s, rope_s, dk_s, dv_s, hgrn_s, ret_s)
```

```python
import functools
import math

import numpy as np
import jax
import jax.numpy as jnp
from jax import lax
from jax.experimental import pallas as pl
from jax.experimental.pallas import tpu as pltpu

F32 = jnp.float32
BF16 = jnp.bfloat16

D_MODEL = 2048
DEPTH = 2
PAGE_SIZE = 128
N_BRANCH = 4
BRANCH_W = D_MODEL // 4
H_A, D_NOPE, D_ROPE, DV_A = 4, 128, 64, 128
Q_LORA, KV_LORA = 384, 256
MLA_SCALE = (D_NOPE + D_ROPE) ** -0.5
H_B, DK_B, DV_B = 4, 64, 128
DIFF_SCALE = DK_B ** -0.5
H_C, DK_C, DV_C = 4, 128, 128
H_D, DK_D, DV_D = 4, 64, 128
D_FF = 5632
N_BUCKETS, T5_MAX_EXACT, T5_MAX_DIST = 32, 16, 128
ROPE_THETA = 10000.0
EPS = 1e-6

LANES = 128
VMEM_LIMIT_BYTES = 56 * 2 ** 20

TM_DENSE = 1024
TN_DENSE = 512
TM_PREP = 256
T_ATTN = 512
T_REC = 128
SUB = 16
PAGES_PER_STEP = 16
NEG_INF = float("-inf")


def _params(*sem):
    return pltpu.CompilerParams(dimension_semantics=sem, vmem_limit_bytes=VMEM_LIMIT_BYTES)


def _rms(x, gain):
    return x * lax.rsqrt(jnp.mean(x * x, axis=-1, keepdims=True) + EPS) * gain


def _dot(a, b):
    return jnp.dot(a, b, preferred_element_type=F32)


def _dot_nt(a, b):
    return lax.dot_general(a, b, (((1,), (1,)), ((), ())), preferred_element_type=F32)


def _dot_tn(a, b):
    return lax.dot_general(a, b, (((0,), (0,)), ((), ())), preferred_element_type=F32)


def _norm_matmul_kernel(x_ref, g_ref, w_ref, o_ref, h_ref):
    @pl.when(pl.program_id(1) == 0)
    def _():
        h_ref[...] = _rms(x_ref[...], g_ref[...]).astype(BF16)

    o_ref[...] = _dot(h_ref[...], w_ref[...]).astype(o_ref.dtype)


def _norm_matmul(x, gain, w, tn):
    m, k = x.shape
    n = w.shape[1]
    tm = min(TM_DENSE, m)
    assert m % tm == 0 and n % tn == 0
    return pl.pallas_call(
        _norm_matmul_kernel,
        grid=(m // tm, n // tn),
        in_specs=[pl.BlockSpec((tm, k), lambda i, j: (i, 0)),
                  pl.BlockSpec((1, k), lambda i, j: (0, 0)),
                  pl.BlockSpec((k, tn), lambda i, j: (0, j))],
        out_specs=pl.BlockSpec((tm, tn), lambda i, j: (i, j)),
        out_shape=jax.ShapeDtypeStruct((m, n), F32),
        scratch_shapes=[pltpu.VMEM((tm, k), BF16)],
        compiler_params=_params("parallel", "arbitrary"),
    )(x, gain.reshape(1, k), w)


def _ffn_up_kernel(x_ref, g_ref, wg_ref, wu_ref, o_ref, h_ref):
    @pl.when(pl.program_id(1) == 0)
    def _():
        h_ref[...] = _rms(x_ref[...], g_ref[...]).astype(BF16)

    h = h_ref[...]
    a = _dot(h, wg_ref[...])
    o_ref[...] = (a * jax.nn.sigmoid(a) * _dot(h, wu_ref[...])).astype(o_ref.dtype)


def _ffn_up(x, gain, wg, wu):
    m, k = x.shape
    n = wg.shape[1]
    tm, tn = min(TM_DENSE, m), TN_DENSE
    assert m % tm == 0 and n % tn == 0
    return pl.pallas_call(
        _ffn_up_kernel,
        grid=(m // tm, n // tn),
        in_specs=[pl.BlockSpec((tm, k), lambda i, j: (i, 0)),
                  pl.BlockSpec((1, k), lambda i, j: (0, 0)),
                  pl.BlockSpec((k, tn), lambda i, j: (0, j)),
                  pl.BlockSpec((k, tn), lambda i, j: (0, j))],
        out_specs=pl.BlockSpec((tm, tn), lambda i, j: (i, j)),
        out_shape=jax.ShapeDtypeStruct((m, n), BF16),
        scratch_shapes=[pltpu.VMEM((tm, k), BF16)],
        compiler_params=_params("parallel", "arbitrary"),
    )(x, gain.reshape(1, k), wg, wu)


def _matmul_res_kernel(a_ref, w_ref, r_ref, o_ref):
    o_ref[...] = r_ref[...] + _dot(a_ref[...], w_ref[...])


def _matmul_res(a, w, res, tm):
    m, k = a.shape
    n = w.shape[1]
    tn = TN_DENSE
    assert m % tm == 0 and n % tn == 0
    return pl.pallas_call(
        _matmul_res_kernel,
        grid=(m // tm, n // tn),
        in_specs=[pl.BlockSpec((tm, k), lambda i, j: (i, 0)),
                  pl.BlockSpec((k, tn), lambda i, j: (0, j)),
                  pl.BlockSpec((tm, tn), lambda i, j: (i, j))],
        out_specs=pl.BlockSpec((tm, tn), lambda i, j: (i, j)),
        out_shape=jax.ShapeDtypeStruct((m, n), F32),
        compiler_params=_params("parallel", "arbitrary"),
    )(a, w, res)


def _merge_kernel(oa_ref, ob_ref, oc_ref, od_ref, wb_ref, ga_ref, gb_ref, gc_ref, gd_ref, o_ref):
    acc = None
    for n, (o, g) in enumerate(((oa_ref, ga_ref), (ob_ref, gb_ref), (oc_ref, gc_ref), (od_ref, gd_ref))):
        t = jax.nn.sigmoid(g[...]) * _dot(o[...], wb_ref[n])
        acc = t if acc is None else acc + t
    o_ref[...] = acc.astype(o_ref.dtype)


def _merge(branches, w_branch, gate_logits):
    m = branches[0].shape[0]
    tm, tn = min(TM_DENSE, m), TN_DENSE
    nj = D_MODEL // tn
    bspec = pl.BlockSpec((tm, BRANCH_W), lambda i, j: (i, 0))
    gspecs = [pl.BlockSpec((tm, tn), functools.partial(lambda i, j, n: (i, n * nj + j), n=n))
              for n in range(N_BRANCH)]
    return pl.pallas_call(
        _merge_kernel,
        grid=(m // tm, nj),
        in_specs=[bspec] * N_BRANCH + [pl.BlockSpec((N_BRANCH, BRANCH_W, tn), lambda i, j: (0, 0, j))] + gspecs,
        out_specs=pl.BlockSpec((tm, tn), lambda i, j: (i, j)),
        out_shape=jax.ShapeDtypeStruct((m, D_MODEL), BF16),
        compiler_params=_params("parallel", "arbitrary"),
    )(*branches, w_branch, gate_logits, gate_logits, gate_logits, gate_logits)


def _rope64(x, cos, sin_signed):
    w = x.shape[-1]
    lane = lax.broadcasted_iota(jnp.int32, x.shape, x.ndim - 1)
    first_half = (lane % D_ROPE) < (D_ROPE // 2)
    partner = jnp.where(first_half, pltpu.roll(x, w - D_ROPE // 2, x.ndim - 1), pltpu.roll(x, D_ROPE // 2, x.ndim - 1))
    return x * cos + partner * sin_signed


def _rope_tables(pos, width):
    half = D_ROPE // 2
    inv = ROPE_THETA ** (-jnp.arange(half, dtype=F32) / half)
    ang = pos.astype(F32)[:, None] * inv[None, :]
    cos, sin = jnp.cos(ang), jnp.sin(ang)
    reps = width // D_ROPE
    return (jnp.tile(jnp.concatenate([cos, cos], axis=-1), (1, reps)),
            jnp.tile(jnp.concatenate([-sin, sin], axis=-1), (1, reps)))


def _mla_prep_kernel(z_ref, gq_ref, wn_ref, wr_ref, ghn_ref, ghr_ref, wuk_ref, gkv_ref, gkr_ref, cos_ref, sin_ref,
                     qcat_ref, kcat_ref, ckv_ref, krope_ref):
    z = z_ref[...]
    qn = _rms(z[:, :Q_LORA], gq_ref[...]).astype(BF16)
    qr = _dot(qn, wr_ref[...])
    lane = lax.broadcasted_iota(jnp.int32, qr.shape, 1)
    qr2 = qr * qr
    scale_lanes = jnp.zeros_like(qr)
    q_lat = []
    for h in range(H_A):
        in_head = (lane // D_ROPE) == h
        qh = _dot(qn, wn_ref[h])
        ss = jnp.sum(qh * qh, axis=-1, keepdims=True) + jnp.sum(jnp.where(in_head, qr2, 0.0), axis=-1, keepdims=True)
        r = lax.rsqrt(ss * (1.0 / (D_NOPE + D_ROPE)) + EPS)
        scale_lanes = jnp.where(in_head, r, scale_lanes)
        q_lat.append(_dot((qh * r * ghn_ref[...]).astype(BF16), wuk_ref[h]))
    cos, sin = cos_ref[...], sin_ref[...]
    qrr = _rope64(qr * scale_lanes * ghr_ref[...], cos, sin)
    for h in range(H_A):
        in_head = (lane // D_ROPE) == h
        qcat_ref[h] = jnp.concatenate([q_lat[h] * MLA_SCALE, jnp.where(in_head, qrr, 0.0) * MLA_SCALE],
                                      axis=-1).astype(BF16)
    ckv = _rms(z[:, Q_LORA:Q_LORA + KV_LORA], gkv_ref[...])
    krr = _rope64(_rms(z[:, Q_LORA + KV_LORA:], gkr_ref[...]), cos, sin)
    ckv_ref[...] = ckv
    krope_ref[...] = krr[:, :D_ROPE]
    kcat_ref[...] = jnp.concatenate([ckv, krr], axis=-1).astype(BF16)


def _mla_prep(z_a, p, cos, sin):
    m = z_a.shape[0]
    tm = TM_PREP
    wa = z_a.shape[1]
    wr = H_A * D_ROPE
    const = lambda *shape: pl.BlockSpec(shape, lambda i: (0,) * len(shape))
    return pl.pallas_call(
        _mla_prep_kernel,
        grid=(m // tm,),
        in_specs=[pl.BlockSpec((tm, wa), lambda i: (i, 0)),
                  const(1, Q_LORA), const(H_A, Q_LORA, D_NOPE), const(Q_LORA, wr), const(1, D_NOPE), const(1, wr),
                  const(H_A, D_NOPE, KV_LORA), const(1, KV_LORA), const(1, wr),
                  pl.BlockSpec((tm, wr), lambda i: (i, 0)), pl.BlockSpec((tm, wr), lambda i: (i, 0))],
        out_specs=[pl.BlockSpec((H_A, tm, KV_LORA + wr), lambda i: (0, i, 0)),
                   pl.BlockSpec((tm, KV_LORA + wr), lambda i: (i, 0)),
                   pl.BlockSpec((tm, KV_LORA), lambda i: (i, 0)),
                   pl.BlockSpec((tm, D_ROPE), lambda i: (i, 0))],
        out_shape=[jax.ShapeDtypeStruct((H_A, m, KV_LORA + wr), BF16),
                   jax.ShapeDtypeStruct((m, KV_LORA + wr), BF16),
                   jax.ShapeDtypeStruct((m, KV_LORA), F32),
                   jax.ShapeDtypeStruct((m, D_ROPE), F32)],
        compiler_params=_params("parallel"),
    )(z_a, p["g_q"], p["w_nope"], p["w_rope"], p["g_hn"], p["g_hr"], p["w_ukT"], p["g_kv"], p["g_kr"], cos, sin)


def _group_rms64(x, gain_tiled):
    lane = lax.broadcasted_iota(jnp.int32, x.shape, 1)
    x2 = x * x
    scale = jnp.zeros_like(x)
    for g in range(x.shape[1] // DK_B):
        in_g = (lane // DK_B) == g
        ss = jnp.sum(jnp.where(in_g, x2, 0.0), axis=-1, keepdims=True)
        scale = jnp.where(in_g, lax.rsqrt(ss * (1.0 / DK_B) + EPS), scale)
    return x * scale * gain_tiled


def _diff_prep_kernel(z_ref, gq_ref, gk_ref, q_ref, k_ref, v_ref):
    z = z_ref[...]
    nq = H_B * 2 * DK_B
    q_ref[...] = (_group_rms64(z[:, :nq], gq_ref[...]) * DIFF_SCALE).astype(BF16)
    k_ref[...] = _group_rms64(z[:, nq:nq + 2 * DK_B], gk_ref[...])
    v_ref[...] = z[:, nq + 2 * DK_B:]


def _diff_prep(z_b, gq_tiled, gk_tiled):
    m, wb = z_b.shape
    tm = TM_PREP
    nq = H_B * 2 * DK_B
    return pl.pallas_call(
        _diff_prep_kernel,
        grid=(m // tm,),
        in_specs=[pl.BlockSpec((tm, wb), lambda i: (i, 0)),
                  pl.BlockSpec((1, nq), lambda i: (0, 0)), pl.BlockSpec((1, 2 * DK_B), lambda i: (0, 0))],
        out_specs=[pl.BlockSpec((tm, nq), lambda i: (i, 0)),
                   pl.BlockSpec((tm, 2 * DK_B), lambda i: (i, 0)),
                   pl.BlockSpec((tm, DV_B), lambda i: (i, 0))],
        out_shape=[jax.ShapeDtypeStruct((m, nq), BF16),
                   jax.ShapeDtypeStruct((m, 2 * DK_B), F32),
                   jax.ShapeDtypeStruct((m, DV_B), F32)],
        compiler_params=_params("parallel"),
    )(z_b, gq_tiled, gk_tiled)


def _softmax_step(s, m_prev, l_prev):
    reps = s.shape[1] // LANES
    m_next = jnp.maximum(m_prev, jnp.max(s, axis=-1, keepdims=True))
    alpha = jnp.exp(m_prev - m_next)
    p = jnp.exp(s - jnp.concatenate([m_next] * reps, axis=1))
    l_next = alpha * l_prev + jnp.sum(p, axis=-1, keepdims=True)
    return p, m_next, l_next, alpha


def _widen(x, width):
    return x if width == LANES else jnp.concatenate([x] * (width // LANES), axis=1)


def _mla_prompt_kernel(q_ref, k_ref, wuv_ref, o_ref, m_ref, l_ref, acc_ref):
    i, j = pl.program_id(0), pl.program_id(1)
    t = k_ref.shape[0]

    @pl.when(j == 0)
    def _():
        m_ref[...] = jnp.full(m_ref.shape, NEG_INF, F32)
        l_ref[...] = jnp.zeros(l_ref.shape, F32)
        acc_ref[...] = jnp.zeros(acc_ref.shape, F32)

    @pl.when(j <= i)
    def _():
        k = k_ref[...]
        v = k[:, :KV_LORA]
        row = lax.broadcasted_iota(jnp.int32, (t, t), 0)
        col = lax.broadcasted_iota(jnp.int32, (t, t), 1)
        visible = col <= row + jnp.where(j < i, t, 0)
        for h in range(H_A):
            s = jnp.where(visible, _dot_nt(q_ref[h], k), NEG_INF)
            p, m_next, l_next, alpha = _softmax_step(s, m_ref[h], l_ref[h])
            m_ref[h] = m_next
            l_ref[h] = l_next
            acc_ref[h] = acc_ref[h] * _widen(alpha, KV_LORA) + _dot(p.astype(BF16), v)

    @pl.when(j == i)
    def _():
        for h in range(H_A):
            o_lat = acc_ref[h] / _widen(l_ref[h], KV_LORA)
            o_ref[:, h * DV_A:(h + 1) * DV_A] = _dot(o_lat.astype(BF16), wuv_ref[h]).astype(o_ref.dtype)


def _mla_prompt(qcat, kcat, w_uv, t_len):
    t = T_ATTN
    n = t_len // t
    dk = kcat.shape[1]
    return pl.pallas_call(
        _mla_prompt_kernel,
        grid=(n, n),
        in_specs=[pl.BlockSpec((H_A, t, dk), lambda i, j: (0, i, 0)),
                  pl.BlockSpec((t, dk), lambda i, j: (jnp.minimum(i, j), 0)),
                  pl.BlockSpec((H_A, KV_LORA, DV_A), lambda i, j: (0, 0, 0))],
        out_specs=pl.BlockSpec((t, BRANCH_W), lambda i, j: (i, 0)),
        out_shape=jax.ShapeDtypeStruct((t_len, BRANCH_W), BF16),
        scratch_shapes=[pltpu.VMEM((H_A, t, LANES), F32), pltpu.VMEM((H_A, t, LANES), F32),
                        pltpu.VMEM((H_A, t, KV_LORA), F32)],
        compiler_params=_params("parallel", "arbitrary"),
    )(qcat, kcat, w_uv)


def _diff_finish(acc0, l0, acc1, l1, lam, gain, out_scale):
    o = acc0 / l0 - lam * (acc1 / l1)
    return _rms(o, gain) * out_scale


def _diff_prompt_kernel(lam_ref, q_ref, k_ref, v_ref, bias_ref, g_ref, o_ref, m_ref, l_ref, acc_ref, *, out_scale):
    i, j = pl.program_id(0), pl.program_id(1)
    t = k_ref.shape[0]

    @pl.when(j == 0)
    def _():
        m_ref[...] = jnp.full(m_ref.shape, NEG_INF, F32)
        l_ref[...] = jnp.zeros(l_ref.shape, F32)
        acc_ref[...] = jnp.zeros(acc_ref.shape, F32)

    @pl.when(j <= i)
    def _():
        k = k_ref[...]
        v = v_ref[...]
        row = lax.broadcasted_iota(jnp.int32, (t, t), 0)
        col = lax.broadcasted_iota(jnp.int32, (t, t), 1)
        visible = col <= row + jnp.where(j < i, t, 0)
        lane = lax.broadcasted_iota(jnp.int32, (t, 2 * DK_B), 1)
        for h in range(H_B):
            qh = q_ref[:, h * 2 * DK_B:(h + 1) * 2 * DK_B]
            bias = bias_ref[0, h]
            for mp in range(2):
                qm = jnp.where((lane // DK_B) == mp, qh, jnp.zeros_like(qh))
                s = jnp.where(visible, _dot_nt(qm, k) + bias, NEG_INF)
                idx = mp * H_B + h
                p, m_next, l_next, alpha = _softmax_step(s, m_ref[idx], l_ref[idx])
                m_ref[idx] = m_next
                l_ref[idx] = l_next
                acc_ref[idx] = acc_ref[idx] * alpha + _dot(p.astype(BF16), v)

    @pl.when(j == i)
    def _():
        lam = lam_ref[0]
        for h in range(H_B):
            o = _diff_finish(acc_ref[h], l_ref[h], acc_ref[H_B + h], l_ref[H_B + h], lam, g_ref[...], out_scale)
            o_ref[:, h * DV_B:(h + 1) * DV_B] = o.astype(o_ref.dtype)


def _diff_prompt(qb, kb, vb, bias_patches, lam, g_out, out_scale, t_len):
    t = T_ATTN
    n = t_len // t
    return pl.pallas_call(
        functools.partial(_diff_prompt_kernel, out_scale=out_scale),
        grid_spec=pltpu.PrefetchScalarGridSpec(
            num_scalar_prefetch=1,
            grid=(n, n),
            in_specs=[pl.BlockSpec((t, H_B * 2 * DK_B), lambda i, j, lam: (i, 0)),
                      pl.BlockSpec((t, 2 * DK_B), lambda i, j, lam: (jnp.minimum(i, j), 0)),
                      pl.BlockSpec((t, DV_B), lambda i, j, lam: (jnp.minimum(i, j), 0)),
                      pl.BlockSpec((1, H_B, t, t), lambda i, j, lam: (jnp.clip(j - i + 2, 0, 2), 0, 0, 0)),
                      pl.BlockSpec((1, DV_B), lambda i, j, lam: (0, 0))],
            out_specs=pl.BlockSpec((t, BRANCH_W), lambda i, j, lam: (i, 0)),
            scratch_shapes=[pltpu.VMEM((2 * H_B, t, LANES), F32), pltpu.VMEM((2 * H_B, t, LANES), F32),
                            pltpu.VMEM((2 * H_B, t, DV_B), F32)]),
        out_shape=jax.ShapeDtypeStruct((t_len, BRANCH_W), BF16),
        compiler_params=_params("parallel", "arbitrary"),
    )(lam, qb, kb, vb, bias_patches, g_out)


def _new_token_mask(rows, t_new):
    row = lax.broadcasted_iota(jnp.int32, (rows, PAGE_SIZE), 0)
    col = lax.broadcasted_iota(jnp.int32, (rows, PAGE_SIZE), 1)
    return col <= (row % t_new)


def _mla_sample_kernel(pt_ref, ql_ref, qr_ref, nl_ref, nr_ref, wuv_ref, *rest, n_pages, t_new):
    lat_refs = rest[:n_pages]
    rope_refs = rest[n_pages:2 * n_pages]
    o_ref, m_ref, l_ref, acc_ref = rest[2 * n_pages:]
    j = pl.program_id(1)
    ql, qr = ql_ref[0], qr_ref[0]

    @pl.when(j == 0)
    def _():
        m_ref[...] = jnp.full(m_ref.shape, NEG_INF, F32)
        l_ref[...] = jnp.zeros(l_ref.shape, F32)
        acc_ref[...] = jnp.zeros(acc_ref.shape, F32)

    def update(lats, ropes, mask):
        lat_bf = [x.astype(BF16) for x in lats]
        s = jnp.concatenate([_dot_nt(ql, lb) + _dot_nt(qr, r.astype(BF16)) for lb, r in zip(lat_bf, ropes)], axis=1)
        if mask is not None:
            s = jnp.where(mask, s, NEG_INF)
        p, m_next, l_next, alpha = _softmax_step(s, m_ref[...], l_ref[...])
        m_ref[...] = m_next
        l_ref[...] = l_next
        pv = None
        for n, lb in enumerate(lat_bf):
            t = _dot(p[:, n * PAGE_SIZE:(n + 1) * PAGE_SIZE].astype(BF16), lb)
            pv = t if pv is None else pv + t
        acc_ref[...] = acc_ref[...] * _widen(alpha, KV_LORA) + pv

    update([r[0, 0] for r in lat_refs], [r[0, 0] for r in rope_refs], None)

    @pl.when(j == pl.num_programs(1) - 1)
    def _():
        update([nl_ref[0]], [nr_ref[0]], _new_token_mask(ql.shape[0], t_new))
        o_lat = acc_ref[...] / _widen(l_ref[...], KV_LORA)
        for h in range(H_A):
            o_ref[0, :, h * DV_A:(h + 1) * DV_A] = _dot(o_lat[h * t_new:(h + 1) * t_new].astype(BF16),
                                                      wuv_ref[h]).astype(o_ref.dtype)


def _mla_sample(page_table, q_lat, q_rope, new_lat, new_rope, w_uv, cache_lat, cache_rope, li):
    b, rows, _ = q_lat.shape
    t_new = rows // H_A
    n_pages = PAGES_PER_STEP
    steps = page_table.shape[1] // n_pages
    page_spec = lambda width, n: pl.BlockSpec(
        (1, 1, PAGE_SIZE, width), functools.partial(lambda bi, j, pt, n: (li, pt[bi, j * n_pages + n], 0, 0), n=n))
    per_seq = lambda r, w: pl.BlockSpec((1, r, w), lambda bi, j, pt: (bi, 0, 0))
    return pl.pallas_call(
        functools.partial(_mla_sample_kernel, n_pages=n_pages, t_new=t_new),
        grid_spec=pltpu.PrefetchScalarGridSpec(
            num_scalar_prefetch=1,
            grid=(b, steps),
            in_specs=[per_seq(rows, KV_LORA), per_seq(rows, D_ROPE), per_seq(PAGE_SIZE, KV_LORA),
                      per_seq(PAGE_SIZE, D_ROPE),
                      pl.BlockSpec((H_A, KV_LORA, DV_A), lambda bi, j, pt: (0, 0, 0))]
            + [page_spec(KV_LORA, n) for n in range(n_pages)]
            + [page_spec(D_ROPE, n) for n in range(n_pages)],
            out_specs=pl.BlockSpec((1, t_new, BRANCH_W), lambda bi, j, pt: (bi, 0, 0)),
            scratch_shapes=[pltpu.VMEM((rows, LANES), F32), pltpu.VMEM((rows, LANES), F32),
                            pltpu.VMEM((rows, KV_LORA), F32)]),
        out_shape=jax.ShapeDtypeStruct((b, t_new, BRANCH_W), F32),
        compiler_params=_params("parallel", "arbitrary"),
    )(page_table, q_lat, q_rope, new_lat, new_rope, w_uv, *([cache_lat] * n_pages), *([cache_rope] * n_pages))


def _diff_sample_kernel(pt_ref, lam_ref, q_ref, nk_ref, nv_ref, bias_ref, nbias_ref, g_ref, *rest,
                        n_pages, t_new, out_scale):
    k_refs = rest[:n_pages]
    v_refs = rest[n_pages:2 * n_pages]
    o_ref, m_ref, l_ref, acc_ref = rest[2 * n_pages:]
    j = pl.program_id(1)
    q = q_ref[0]

    @pl.when(j == 0)
    def _():
        m_ref[...] = jnp.full(m_ref.shape, NEG_INF, F32)
        l_ref[...] = jnp.zeros(l_ref.shape, F32)
        acc_ref[...] = jnp.zeros(acc_ref.shape, F32)

    def update(ks, vs, bias, mask):
        s = jnp.concatenate([_dot_nt(q, kk.astype(BF16)) for kk in ks], axis=1)
        s = s + jnp.concatenate([bias, bias], axis=0)
        if mask is not None:
            s = jnp.where(mask, s, NEG_INF)
        p, m_next, l_next, alpha = _softmax_step(s, m_ref[...], l_ref[...])
        m_ref[...] = m_next
        l_ref[...] = l_next
        pv = None
        for n, vv in enumerate(vs):
            t = _dot(p[:, n * PAGE_SIZE:(n + 1) * PAGE_SIZE].astype(BF16), vv.astype(BF16))
            pv = t if pv is None else pv + t
        acc_ref[...] = acc_ref[...] * alpha + pv

    update([r[0, 0] for r in k_refs], [r[0, 0] for r in v_refs], bias_ref[...], None)

    @pl.when(j == pl.num_programs(1) - 1)
    def _():
        update([nk_ref[0]], [nv_ref[0]], nbias_ref[...], _new_token_mask(q.shape[0], t_new))
        half = H_B * t_new
        acc, l = acc_ref[...], l_ref[...]
        o = _diff_finish(acc[:half], l[:half], acc[half:], l[half:], lam_ref[0], g_ref[...], out_scale)
        for h in range(H_B):
            o_ref[0, :, h * DV_B:(h + 1) * DV_B] = o[h * t_new:(h + 1) * t_new].astype(o_ref.dtype)


def _diff_sample(page_table, lam, q_pad, new_k, new_v, bias, new_bias, g_out, out_scale, cache_k, cache_v, li):
    b, rows, _ = q_pad.shape
    t_new = rows // (2 * H_B)
    n_pages = PAGES_PER_STEP
    steps = page_table.shape[1] // n_pages
    page_spec = lambda n: pl.BlockSpec(
        (1, 1, PAGE_SIZE, 2 * DK_B),
        functools.partial(lambda bi, j, pt, lm, n: (li, pt[bi, j * n_pages + n], 0, 0), n=n))
    per_seq = lambda r, w: pl.BlockSpec((1, r, w), lambda bi, j, pt, lm: (bi, 0, 0))
    return pl.pallas_call(
        functools.partial(_diff_sample_kernel, n_pages=n_pages, t_new=t_new, out_scale=out_scale),
        grid_spec=pltpu.PrefetchScalarGridSpec(
            num_scalar_prefetch=2,
            grid=(b, steps),
            in_specs=[per_seq(rows, 2 * DK_B), per_seq(PAGE_SIZE, 2 * DK_B), per_seq(PAGE_SIZE, DV_B),
                      pl.BlockSpec((rows // 2, n_pages * PAGE_SIZE), lambda bi, j, pt, lm: (0, j)),
                      pl.BlockSpec((rows // 2, PAGE_SIZE), lambda bi, j, pt, lm: (0, 0)),
                      pl.BlockSpec((1, DV_B), lambda bi, j, pt, lm: (0, 0))]
            + [page_spec(n) for n in range(n_pages)] * 2,
            out_specs=pl.BlockSpec((1, t_new, BRANCH_W), lambda bi, j, pt, lm: (bi, 0, 0)),
            scratch_shapes=[pltpu.VMEM((rows, LANES), F32), pltpu.VMEM((rows, LANES), F32),
                            pltpu.VMEM((rows, DV_B), F32)]),
        out_shape=jax.ShapeDtypeStruct((b, t_new, BRANCH_W), F32),
        compiler_params=_params("parallel", "arbitrary"),
    )(page_table, lam, q_pad, new_k, new_v, bias, new_bias, g_out, *([cache_k] * n_pages), *([cache_v] * n_pages))


def _cumsum_rows(x):
    n = x.shape[0]
    row = lax.broadcasted_iota(jnp.int32, x.shape, 0)
    sh = 1
    while sh < n:
        x = x + jnp.where(row >= sh, pltpu.roll(x, sh, 0), 0.0)
        sh *= 2
    return x


def _group_row(x, group, j):
    n, w = x.shape
    if group == n:
        return jnp.broadcast_to(x[j:j + 1, :], (n, w))
    x3 = x.reshape(n // group, group, w)
    return jnp.broadcast_to(x3[:, j:j + 1, :], x3.shape).reshape(n, w)


def _hgrn_gates(cq, cf, a_lb, c_lb, oml):
    q = cq * jax.nn.sigmoid(cq)
    log_sig = jnp.minimum(cf, 0.0) - jnp.log1p(jnp.exp(-jnp.abs(cf)))
    x = c_lb + log_sig
    g = jnp.maximum(a_lb, x) + jnp.log1p(jnp.exp(-jnp.abs(a_lb - x)))
    k = oml * jax.nn.sigmoid(-cf)
    return q, k, g


def _hgrn_tile(q, k, v, g, st, sub):
    n = q.shape[0]
    b = _cumsum_rows(g)
    row = lax.broadcasted_iota(jnp.int32, (n, 1), 0)
    o = _dot_nt((q * jnp.exp(b)).astype(BF16), st.astype(BF16))
    for j in range(sub):
        e = jnp.where((row % sub) >= j, b - _group_row(b, sub, j), NEG_INF)
        r = jnp.sum(q * _group_row(k, sub, j) * jnp.exp(e), axis=-1, keepdims=True)
        o = o + r * _group_row(v, sub, j)
    half = sub
    while half < n:
        upper = (row % (2 * half)) >= half
        bb = _group_row(b, 2 * half, half - 1)
        qe = q * jnp.exp(jnp.where(upper, b - bb, NEG_INF))
        ke = k * jnp.exp(jnp.where(upper, NEG_INF, bb - b))
        s = _dot_nt(qe.astype(BF16), ke.astype(BF16))
        trow = lax.broadcasted_iota(jnp.int32, (n, n), 0) // (2 * half)
        tcol = lax.broadcasted_iota(jnp.int32, (n, n), 1) // (2 * half)
        o = o + _dot(jnp.where(trow == tcol, s, 0.0).astype(BF16), v.astype(BF16))
        half *= 2
    b_last = b[n - 1:n, :]
    kd = k * jnp.exp(b_last - b)
    st_next = st * jnp.exp(b_last) + _dot_tn(v.astype(BF16), kd.astype(BF16))
    return o, st_next


def _hgrn_prompt_kernel(cq_ref, cf_ref, ci_ref, cg_ref, a_ref, c_ref, oml_ref, go_ref, o_ref, s_ref, st_ref):
    t = pl.program_id(1)

    @pl.when(t == 0)
    def _():
        st_ref[...] = jnp.zeros(st_ref.shape, F32)

    q, k, g = _hgrn_gates(cq_ref[...], cf_ref[...], a_ref[...], c_ref[...], oml_ref[...])
    o, st_next = _hgrn_tile(q, k, ci_ref[...], g, st_ref[...], SUB)
    st_ref[...] = st_next
    o_ref[...] = (_rms(o, go_ref[...]) * jax.nn.sigmoid(cg_ref[...])).astype(o_ref.dtype)

    @pl.when(t == pl.num_programs(1) - 1)
    def _():
        s_ref[0] = st_next.T


def _hgrn_prompt(z_c, a_lb, c_lb, oml, g_out, t_len):
    tt = T_REC
    sect = lambda s: pl.BlockSpec((tt, DK_C), functools.partial(lambda h, t, s: (t, s * H_C + h), s=s))
    per_head = pl.BlockSpec((1, DK_C), lambda h, t: (0, h))
    return pl.pallas_call(
        _hgrn_prompt_kernel,
        grid=(H_C, t_len // tt),
        in_specs=[sect(0), sect(1), sect(2), sect(3), per_head, per_head, per_head,
                  pl.BlockSpec((1, DV_C), lambda h, t: (0, 0))],
        out_specs=[pl.BlockSpec((tt, DV_C), lambda h, t: (t, h)),
                   pl.BlockSpec((1, DK_C, DV_C), lambda h, t: (h, 0, 0))],
        out_shape=[jax.ShapeDtypeStruct((t_len, BRANCH_W), BF16),
                   jax.ShapeDtypeStruct((H_C, DK_C, DV_C), F32)],
        scratch_shapes=[pltpu.VMEM((DV_C, DK_C), F32)],
        compiler_params=_params("parallel", "arbitrary"),
    )(z_c, z_c, z_c, z_c, a_lb, c_lb, oml, g_out)


def _pad_rows(x, n):
    return jnp.concatenate([x, jnp.zeros((n - x.shape[0], x.shape[1]), x.dtype)], axis=0)


def _hgrn_sample_kernel(z_ref, s0_ref, a_ref, c_ref, oml_ref, go_ref, o_ref, s_ref):
    z = z_ref[0]
    n = z.shape[0]
    w = H_C * DK_C
    for h in range(H_C):
        lanes = lambda s: slice(s * w + h * DK_C, s * w + (h + 1) * DK_C)
        hl = slice(h * DK_C, (h + 1) * DK_C)
        q, k, g = _hgrn_gates(z[:, lanes(0)], z[:, lanes(1)], a_ref[:, hl], c_ref[:, hl], oml_ref[:, hl])
        v = z[:, lanes(2)]
        s0 = s0_ref[0, 0, h]
        b = _cumsum_rows(g)
        row = lax.broadcasted_iota(jnp.int32, (n, 1), 0)
        o = _dot((q * jnp.exp(b)).astype(BF16), s0.astype(BF16))
        for j in range(n):
            e = jnp.where(row >= j, b - b[j:j + 1, :], NEG_INF)
            r = jnp.sum(q * k[j:j + 1, :] * jnp.exp(e), axis=-1, keepdims=True)
            o = o + r * v[j:j + 1, :]
        b_last = b[n - 1:n, :]
        kd = _pad_rows((k * jnp.exp(b_last - b)).astype(BF16), LANES)
        st_next = s0.T * jnp.exp(b_last) + _dot_tn(_pad_rows(v.astype(BF16), LANES), kd)
        s_ref[0, h] = st_next.T
        o_ref[0, :, hl] = (_rms(o, go_ref[...]) * jax.nn.sigmoid(z[:, lanes(3)])).astype(o_ref.dtype)


def _hgrn_sample(z_c, state, li, a_lb, c_lb, oml, g_out):
    b, t_new, w = z_c.shape
    vec = pl.BlockSpec((1, H_C * DK_C), lambda i: (0, 0))
    return pl.pallas_call(
        _hgrn_sample_kernel,
        grid=(b,),
        in_specs=[pl.BlockSpec((1, t_new, w), lambda i: (i, 0, 0)),
                  pl.BlockSpec((1, 1, H_C, DK_C, DV_C), lambda i: (li, i, 0, 0, 0)),
                  vec, vec, vec, pl.BlockSpec((1, DV_C), lambda i: (0, 0))],
        out_specs=[pl.BlockSpec((1, t_new, BRANCH_W), lambda i: (i, 0, 0)),
                   pl.BlockSpec((1, H_C, DK_C, DV_C), lambda i: (i, 0, 0, 0))],
        out_shape=[jax.ShapeDtypeStruct((b, t_new, BRANCH_W), F32),
                   jax.ShapeDtypeStruct((b, H_C, DK_C, DV_C), F32)],
        compiler_params=_params("parallel"),
    )(z_c, state, a_lb, c_lb, oml, g_out)


def _ret_log_decay():
    return np.log1p(-np.exp2(-5.0 - np.arange(H_D, dtype=np.float64)))


def _ret_tables(n, block):
    lg = _ret_log_decay()
    t = np.arange(n)
    loc = t % block
    same = (t[:, None] // block) == (t[None, :] // block)
    diff = loc[:, None] - loc[None, :]
    dmat = np.where(same & (diff >= 0), np.exp(lg[:, None, None] * np.maximum(diff, 0)[None]), 0.0)
    per_lane = lambda f: np.repeat(f, DK_D, axis=1)
    dq = per_lane(np.exp(lg[None, :] * (loc[:, None] + 1)))
    dk = per_lane(np.exp(lg[None, :] * (block - 1 - loc[:, None])))
    dend = per_lane(np.exp(lg[None, :] * block))
    return tuple(jnp.asarray(a, F32) for a in (dmat, dq, dk, dend))


def _ret_prompt_kernel(rq_ref, rk_ref, rv_ref, rg_ref, cos_ref, sin_ref, dmat_ref, dq_ref, dk_ref, dend_ref, go_ref,
                       o_ref, s_ref, st_ref):
    t = pl.program_id(0)

    @pl.when(t == 0)
    def _():
        st_ref[...] = jnp.zeros(st_ref.shape, F32)

    cos, sin = cos_ref[...], sin_ref[...]
    q = _rope64(rq_ref[...], cos, sin)
    k = _rope64(rk_ref[...], cos, sin) * (DK_D ** -0.5)
    lane = lax.broadcasted_iota(jnp.int32, q.shape, 1)
    k_bf = k.astype(BF16)
    qd = q * dq_ref[...]
    kd = k * dk_ref[...]
    for h in range(H_D):
        in_head = (lane // DK_D) == h
        vl = slice(h * DV_D, (h + 1) * DV_D)
        v = rv_ref[:, vl].astype(BF16)
        st = st_ref[h]
        s = _dot_nt(jnp.where(in_head, q, 0.0).astype(BF16), k_bf) * dmat_ref[h]
        o = _dot(s.astype(BF16), v) + _dot_nt(jnp.where(in_head, qd, 0.0).astype(BF16), st.astype(BF16))
        st_next = st * dend_ref[...] + _dot_tn(v, jnp.where(in_head, kd, 0.0).astype(BF16))
        st_ref[h] = st_next
        o_ref[:, vl] = (_rms(o, go_ref[...]) * (rg_ref[:, vl] * jax.nn.sigmoid(rg_ref[:, vl]))).astype(o_ref.dtype)

        @pl.when(t == pl.num_programs(0) - 1)
        def _():
            s_ref[h] = st_next.T[h * DK_D:(h + 1) * DK_D, :]


def _ret_prompt(z_d, cos, sin, g_out, t_len):
    tt = T_REC
    wq = H_D * DK_D
    wv = H_D * DV_D
    dmat, dq, dk, dend = _ret_tables(tt, tt)
    const = lambda *shape: pl.BlockSpec(shape, lambda t: (0,) * len(shape))
    return pl.pallas_call(
        _ret_prompt_kernel,
        grid=(t_len // tt,),
        in_specs=[pl.BlockSpec((tt, wq), lambda t: (t, 0)), pl.BlockSpec((tt, wq), lambda t: (t, 1)),
                  pl.BlockSpec((tt, wv), lambda t: (t, 1)), pl.BlockSpec((tt, wv), lambda t: (t, 2)),
                  pl.BlockSpec((tt, wq), lambda t: (t, 0)), pl.BlockSpec((tt, wq), lambda t: (t, 0)),
                  const(H_D, tt, tt), const(tt, wq), const(tt, wq), const(1, wq), const(1, DV_D)],
        out_specs=[pl.BlockSpec((tt, BRANCH_W), lambda t: (t, 0)),
                   pl.BlockSpec((H_D, DK_D, DV_D), lambda t: (0, 0, 0))],
        out_shape=[jax.ShapeDtypeStruct((t_len, BRANCH_W), BF16),
                   jax.ShapeDtypeStruct((H_D, DK_D, DV_D), F32)],
        scratch_shapes=[pltpu.VMEM((H_D, DV_D, wq), F32)],
        compiler_params=_params("arbitrary"),
    )(z_d, z_d, z_d, z_d, cos, sin, dmat, dq, dk, dend, g_out)


def _ret_sample_kernel(z_ref, s0_ref, cos_ref, sin_ref, dmat_ref, dq_ref, dk_ref, dend_ref, go_ref, o_ref, s_ref):
    z = z_ref[0]
    n = z.shape[0]
    wq = H_D * DK_D
    cos, sin = cos_ref[...], sin_ref[...]
    q = _rope64(z[:, :wq], cos, sin)
    k = _rope64(z[:, wq:2 * wq], cos, sin) * (DK_D ** -0.5)
    lane = lax.broadcasted_iota(jnp.int32, q.shape, 1)
    k_pad = _pad_rows(k.astype(BF16), LANES)
    qd = q * dq_ref[...]
    kd = _pad_rows((k * dk_ref[...]).astype(BF16), LANES)
    lane_pad = lax.broadcasted_iota(jnp.int32, kd.shape, 1)
    s_all = s0_ref[0, 0].reshape(wq, DV_D)
    for h in range(H_D):
        in_head = (lane // DK_D) == h
        rows = slice(h * DK_D, (h + 1) * DK_D)
        vl = slice(2 * wq + h * DV_D, 2 * wq + (h + 1) * DV_D)
        gl = slice(2 * wq + H_D * DV_D + h * DV_D, 2 * wq + H_D * DV_D + (h + 1) * DV_D)
        v_pad = _pad_rows(z[:, vl].astype(BF16), LANES)
        s = _dot_nt(jnp.where(in_head, q, 0.0).astype(BF16), k_pad) * dmat_ref[h]
        o = _dot(s.astype(BF16), v_pad) + _dot(jnp.where(in_head, qd, 0.0).astype(BF16), s_all.astype(BF16))
        upd = _dot_tn(jnp.where((lane_pad // DK_D) == h, kd, jnp.zeros_like(kd)), v_pad)
        s_ref[0, h] = s_all[rows] * dend_ref[rows] + upd[rows]
        o_ref[0, :, h * DV_D:(h + 1) * DV_D] = (
            _rms(o, go_ref[...]) * (z[:, gl] * jax.nn.sigmoid(z[:, gl]))).astype(o_ref.dtype)
    del n


def _ret_sample(z_d, state, li, cos, sin, g_out):
    b, t_new, w = z_d.shape
    wq = H_D * DK_D
    dmat, dq, dk, dend = _ret_tables(t_new, t_new)
    dmat = jnp.pad(dmat, ((0, 0), (0, 0), (0, LANES - t_new)))
    const = lambda *shape: pl.BlockSpec(shape, lambda i: (0,) * len(shape))
    return pl.pallas_call(
        _ret_sample_kernel,
        grid=(b,),
        in_specs=[pl.BlockSpec((1, t_new, w), lambda i: (i, 0, 0)),
                  pl.BlockSpec((1, 1, H_D, DK_D, DV_D), lambda i: (li, i, 0, 0, 0)),
                  const(t_new, wq), const(t_new, wq), const(H_D, t_new, LANES), const(t_new, wq), const(t_new, wq),
                  const(wq, 1), const(1, DV_D)],
        out_specs=[pl.BlockSpec((1, t_new, BRANCH_W), lambda i: (i, 0, 0)),
                   pl.BlockSpec((1, H_D, DK_D, DV_D), lambda i: (i, 0, 0, 0))],
        out_shape=[jax.ShapeDtypeStruct((b, t_new, BRANCH_W), F32),
                   jax.ShapeDtypeStruct((b, H_D, DK_D, DV_D), F32)],
        compiler_params=_params("parallel"),
    )(z_d, state, cos, sin, dmat, dq, dk, dend.reshape(wq, 1), g_out)


def _t5_bucket(rel):
    n = jnp.maximum(jnp.asarray(rel, jnp.int32), 0)
    large = T5_MAX_EXACT + (jnp.log(jnp.maximum(n, 1).astype(F32) / T5_MAX_EXACT)
                            / math.log(T5_MAX_DIST / T5_MAX_EXACT)
                            * (N_BUCKETS - T5_MAX_EXACT)).astype(jnp.int32)
    return jnp.where(n < T5_MAX_EXACT, n, jnp.minimum(large, N_BUCKETS - 1))


def _bias_lookup(table, buckets):
    out = jnp.zeros((table.shape[1],) + buckets.shape, F32)
    expand = (slice(None),) + (None,) * buckets.ndim
    for b in range(N_BUCKETS):
        out = jnp.where(buckets[None] == b, table[b][expand], out)
    return out


def _layer_weights(li, w_in, mla_w_qb, mla_w_uk, mla_w_uv, w_branch, w_out, w_gate, w_up, w_down):
    offs = np.cumsum([0, Q_LORA, KV_LORA, D_ROPE, H_B * 2 * DK_B, 2 * DK_B, DV_B, H_C * DK_C, H_C * DK_C, H_C * DV_C,
                      H_C * DV_C, H_D * DK_D, H_D * DK_D, H_D * DV_D, H_D * DV_D, N_BRANCH * D_MODEL])
    w = w_in[li]
    col = lambda a, b: w[:, offs[a]:offs[b]]
    w_a = jnp.concatenate([col(0, 2)] + [col(2, 3)] * H_A, axis=1)
    groups = [w_a, col(3, 6), col(6, 10), col(10, 14), col(14, 15)]
    wqb = mla_w_qb[li]
    return {
        "w_in": [g.astype(BF16) for g in groups],
        "w_nope": jnp.transpose(wqb[:, :, :D_NOPE], (1, 0, 2)).astype(BF16),
        "w_rope": wqb[:, :, D_NOPE:].reshape(Q_LORA, H_A * D_ROPE).astype(BF16),
        "w_ukT": jnp.transpose(mla_w_uk[li], (1, 2, 0)).astype(BF16),
        "w_uv": jnp.transpose(mla_w_uv[li], (1, 0, 2)).astype(BF16),
        "w_branch": w_branch[li].astype(BF16),
        "w_out": w_out[li].astype(BF16),
        "w_gate": w_gate[li].astype(BF16),
        "w_up": w_up[li].astype(BF16),
        "w_down": w_down[li].astype(BF16),
    }


def kernel(x_prompt, x_sample, cache_mla_latent, cache_mla_rope, cache_diff_k, cache_diff_v, state_hgrn, state_ret, page_table, t5_bias, norm_attn, w_in, mla_q_norm, mla_w_qb, mla_q_head_norm, mla_kv_norm, mla_krope_norm, mla_w_uk, mla_w_uv, diff_q_norm, diff_k_norm, diff_lambda, diff_out_norm, hgrn_lb_logits, hgrn_out_norm, ret_out_norm, w_branch, w_out, norm_ffn, w_gate, w_up, w_down):
    n_batch, t_p, _ = x_prompt.shape
    n_seq, t_s, _ = x_sample.shape
    assert n_batch == 1 and t_p % T_ATTN == 0
    past = page_table.shape[1] * PAGE_SIZE
    n_s = n_seq * t_s
    assert (t_p + n_s) % TM_DENSE == 0 and page_table.shape[1] % PAGES_PER_STEP == 0

    x = jnp.concatenate([x_prompt.reshape(t_p, D_MODEL), x_sample.reshape(n_s, D_MODEL)], axis=0)
    pos = jnp.concatenate([jnp.arange(t_p, dtype=jnp.int32), jnp.tile(past + jnp.arange(t_s, dtype=jnp.int32), n_seq)])
    cos, sin = _rope_tables(pos, H_A * D_ROPE)
    cos_s, sin_s = cos[t_p:t_p + t_s], sin[t_p:t_p + t_s]

    ta = np.arange(T_ATTN)
    rel = ta[:, None] - ta[None, :]
    patch_buckets = jnp.stack([_t5_bucket(rel + 2 * T_ATTN), _t5_bucket(rel + T_ATTN), _t5_bucket(rel)])
    bias_patches = jnp.transpose(_bias_lookup(t5_bias, patch_buckets), (1, 0, 2, 3))
    q_pos = past + np.arange(t_s)
    strip_buckets = _t5_bucket(q_pos[:, None] - np.arange(past + PAGE_SIZE)[None, :])
    bias_strip = _bias_lookup(t5_bias, strip_buckets).reshape(H_B * t_s, past + PAGE_SIZE)
    bias_past, bias_new = bias_strip[:, :past], bias_strip[:, past:]

    lb_cum = jnp.cumsum(jax.nn.softmax(hgrn_lb_logits.astype(F32), axis=0), axis=0)
    lb_all = lb_cum - lb_cum[:1]

    cache_k = cache_diff_k.reshape(cache_diff_k.shape[:3] + (2 * DK_B,))
    cache_v = cache_diff_v.reshape(cache_diff_v.shape[:3] + (DV_B,))
    pad_new = lambda a: jnp.pad(a.reshape(n_seq, t_s, a.shape[-1]), ((0, 0), (0, PAGE_SIZE - t_s), (0, 0)))

    outs_p, outs_s = [], []
    for li in range(DEPTH):
        wts = _layer_weights(li, w_in, mla_w_qb, mla_w_uk, mla_w_uv, w_branch, w_out, w_gate, w_up, w_down)
        z_a, z_b, z_c, z_d, z_g = [
            _norm_matmul(x, norm_attn[li], w, tn)
            for w, tn in zip(wts["w_in"], (wts["w_in"][0].shape[1], wts["w_in"][1].shape[1], TN_DENSE, TN_DENSE, TN_DENSE))]

        pa = {"g_q": mla_q_norm[li].reshape(1, -1), "w_nope": wts["w_nope"], "w_rope": wts["w_rope"],
              "g_hn": mla_q_head_norm[li, :D_NOPE].reshape(1, -1),
              "g_hr": jnp.tile(mla_q_head_norm[li, D_NOPE:], H_A).reshape(1, -1), "w_ukT": wts["w_ukT"],
              "g_kv": mla_kv_norm[li].reshape(1, -1), "g_kr": jnp.tile(mla_krope_norm[li], H_A).reshape(1, -1)}
        qcat, kcat, c_kv, k_rope = _mla_prep(z_a, pa, cos, sin)
        o_a_p = _mla_prompt(qcat, kcat, wts["w_uv"], t_p)
        qs = qcat[:, t_p:].reshape(H_A, n_seq, t_s, -1)
        q_lat_s = jnp.transpose(qs[..., :KV_LORA], (1, 0, 2, 3)).reshape(n_seq, H_A * t_s, KV_LORA)
        q_rope_s = jnp.stack([qs[h, :, :, KV_LORA + h * D_ROPE:KV_LORA + (h + 1) * D_ROPE] for h in range(H_A)], axis=1)
        q_rope_s = q_rope_s.reshape(n_seq, H_A * t_s, D_ROPE)
        o_a_s = _mla_sample(page_table, q_lat_s, q_rope_s, pad_new(c_kv[t_p:]), pad_new(k_rope[t_p:]), wts["w_uv"],
                            cache_mla_latent, cache_mla_rope, li)

        lam_init = 0.8 - 0.6 * math.exp(-0.3 * li)
        lv = diff_lambda[li].astype(F32)
        lam = (jnp.exp(jnp.sum(lv[0] * lv[1])) - jnp.exp(jnp.sum(lv[2] * lv[3])) + lam_init).reshape(1)
        qb, kb, vb = _diff_prep(z_b, jnp.tile(diff_q_norm[li], 2 * H_B).reshape(1, -1),
                                jnp.tile(diff_k_norm[li], 2).reshape(1, -1))
        g_diff = diff_out_norm[li].reshape(1, -1)
        o_b_p = _diff_prompt(qb, kb.astype(BF16), vb.astype(BF16), bias_patches, lam, g_diff, 1.0 - lam_init, t_p)
        qbs = qb[t_p:].reshape(n_seq, t_s, H_B, 2, DK_B)
        zero = jnp.zeros_like(qbs[..., 0, :])
        q_pad = jnp.stack([jnp.concatenate([qbs[..., 0, :], zero], axis=-1),
                           jnp.concatenate([zero, qbs[..., 1, :]], axis=-1)], axis=1)
        q_pad = jnp.transpose(q_pad, (0, 1, 3, 2, 4)).reshape(n_seq, 2 * H_B * t_s, 2 * DK_B)
        o_b_s = _diff_sample(page_table, lam, q_pad, pad_new(kb[t_p:]), pad_new(vb[t_p:]), bias_past, bias_new, g_diff,
                             1.0 - lam_init, cache_k, cache_v, li)

        lb = lb_all[li].reshape(1, -1)
        a_lb, c_lb, oml = jnp.log(lb), jnp.log1p(-lb), 1.0 - lb
        g_hgrn = hgrn_out_norm[li].reshape(1, -1)
        o_c_p, s_hgrn_p = _hgrn_prompt(z_c, a_lb, c_lb, oml, g_hgrn, t_p)
        o_c_s, s_hgrn_s = _hgrn_sample(z_c[t_p:].reshape(n_seq, t_s, -1), state_hgrn, li, a_lb, c_lb, oml, g_hgrn)

        g_ret = ret_out_norm[li].reshape(1, -1)
        o_d_p, s_ret_p = _ret_prompt(z_d, cos, sin, g_ret, t_p)
        o_d_s, s_ret_s = _ret_sample(z_d[t_p:].reshape(n_seq, t_s, -1), state_ret, li, cos_s, sin_s, g_ret)

        branches = [jnp.concatenate([op, os_.reshape(n_s, BRANCH_W).astype(BF16)], axis=0)
                    for op, os_ in ((o_a_p, o_a_s), (o_b_p, o_b_s), (o_c_p, o_c_s), (o_d_p, o_d_s))]
        mixed = _merge(branches, wts["w_branch"], z_g)
        x = _matmul_res(mixed, wts["w_out"], x, TM_DENSE)
        u = _ffn_up(x, norm_ffn[li], wts["w_gate"], wts["w_up"])
        x = _matmul_res(u, wts["w_down"], x, TM_DENSE // 2)

        outs_p.append((c_kv[:t_p].reshape(1, t_p, KV_LORA), k_rope[:t_p].reshape(1, t_p, D_ROPE),
                       kb[:t_p].reshape(1, t_p, 1, 2 * DK_B), vb[:t_p].reshape(1, t_p, 1, DV_B),
                       s_hgrn_p[None], s_ret_p[None]))
        outs_s.append((c_kv[t_p:].reshape(n_seq, t_s, KV_LORA), k_rope[t_p:].reshape(n_seq, t_s, D_ROPE),
                       kb[t_p:].reshape(n_seq, t_s, 1, 2 * DK_B), vb[t_p:].reshape(n_seq, t_s, 1, DV_B),
                       s_hgrn_s, s_ret_s))

    stack = lambda outs: [jnp.stack(a) for a in zip(*outs)]
    y_p = x[:t_p].reshape(1, t_p, D_MODEL)
    y_s = x[t_p:].reshape(n_seq, t_s, D_MODEL)
    return (y_p, y_s, *stack(outs_p), *stack(outs_s))
```

```python
import functools
import math

import numpy as np
import jax
import jax.numpy as jnp
from jax import lax
from jax.experimental import pallas as pl
from jax.experimental.pallas import tpu as pltpu

F32 = jnp.float32
BF16 = jnp.bfloat16

D_MODEL = 2048
DEPTH = 2
PAGE_SIZE = 128
N_BRANCH = 4
BRANCH_W = D_MODEL // 4
H_A, D_NOPE, D_ROPE, DV_A = 4, 128, 64, 128
Q_LORA, KV_LORA = 384, 256
MLA_SCALE = (D_NOPE + D_ROPE) ** -0.5
H_B, DK_B, DV_B = 4, 64, 128
DIFF_SCALE = DK_B ** -0.5
H_C, DK_C, DV_C = 4, 128, 128
H_D, DK_D, DV_D = 4, 64, 128
D_FF = 5632
N_BUCKETS, T5_MAX_EXACT, T5_MAX_DIST = 32, 16, 128
ROPE_THETA = 10000.0
EPS = 1e-6
LOG2E = math.log2(math.e)
MLA_QSCALE = MLA_SCALE * LOG2E
DIFF_QSCALE = DIFF_SCALE * LOG2E

LANES = 128
VMEM_LIMIT_BYTES = 56 * 2 ** 20

TM_DENSE = 1024
TN_DENSE = 512
TM_PREP = 256
T_ATTN = 512
T_REC = 128
SUB = 16
PAGES_PER_STEP = 32
SEQS_PER_STEP = 4
T_RET = 512
NEG_INF = float("-inf")


def _params(*sem):
    return pltpu.CompilerParams(dimension_semantics=sem, vmem_limit_bytes=VMEM_LIMIT_BYTES)


def _rms(x, gain):
    return x * lax.rsqrt(jnp.mean(x * x, axis=-1, keepdims=True) + EPS) * gain


def _dot(a, b):
    return jnp.dot(a, b, preferred_element_type=F32)


def _dot_nt(a, b):
    return lax.dot_general(a, b, (((1,), (1,)), ((), ())), preferred_element_type=F32)


def _dot_tn(a, b):
    return lax.dot_general(a, b, (((0,), (0,)), ((), ())), preferred_element_type=F32)


def _norm_matmul_kernel(x_ref, g_ref, w_ref, o_ref, h_ref):
    @pl.when(pl.program_id(1) == 0)
    def _():
        h_ref[...] = _rms(x_ref[...], g_ref[...]).astype(BF16)

    o_ref[...] = _dot(h_ref[...], w_ref[...]).astype(o_ref.dtype)


def _norm_matmul(x, gain, w, tn):
    m, k = x.shape
    n = w.shape[1]
    tm = min(TM_DENSE, m)
    assert m % tm == 0 and n % tn == 0
    return pl.pallas_call(
        _norm_matmul_kernel,
        grid=(m // tm, n // tn),
        in_specs=[pl.BlockSpec((tm, k), lambda i, j: (i, 0)),
                  pl.BlockSpec((1, k), lambda i, j: (0, 0)),
                  pl.BlockSpec((k, tn), lambda i, j: (0, j))],
        out_specs=pl.BlockSpec((tm, tn), lambda i, j: (i, j)),
        out_shape=jax.ShapeDtypeStruct((m, n), F32),
        scratch_shapes=[pltpu.VMEM((tm, k), BF16)],
        compiler_params=_params("parallel", "arbitrary"),
    )(x, gain.reshape(1, k), w)


def _ffn_up_kernel(x_ref, g_ref, wg_ref, wu_ref, o_ref, h_ref):
    @pl.when(pl.program_id(1) == 0)
    def _():
        h_ref[...] = _rms(x_ref[...], g_ref[...]).astype(BF16)

    h = h_ref[...]
    a = _dot(h, wg_ref[...])
    o_ref[...] = (a * jax.nn.sigmoid(a) * _dot(h, wu_ref[...])).astype(o_ref.dtype)


def _ffn_up(x, gain, wg, wu):
    m, k = x.shape
    n = wg.shape[1]
    tm, tn = min(TM_DENSE, m), TN_DENSE
    assert m % tm == 0 and n % tn == 0
    return pl.pallas_call(
        _ffn_up_kernel,
        grid=(m // tm, n // tn),
        in_specs=[pl.BlockSpec((tm, k), lambda i, j: (i, 0)),
                  pl.BlockSpec((1, k), lambda i, j: (0, 0)),
                  pl.BlockSpec((k, tn), lambda i, j: (0, j)),
                  pl.BlockSpec((k, tn), lambda i, j: (0, j))],
        out_specs=pl.BlockSpec((tm, tn), lambda i, j: (i, j)),
        out_shape=jax.ShapeDtypeStruct((m, n), BF16),
        scratch_shapes=[pltpu.VMEM((tm, k), BF16)],
        compiler_params=_params("parallel", "arbitrary"),
    )(x, gain.reshape(1, k), wg, wu)


def _matmul_res_kernel(a_ref, w_ref, r_ref, o_ref):
    o_ref[...] = r_ref[...] + _dot(a_ref[...], w_ref[...])


def _matmul_res(a, w, res, tm):
    m, k = a.shape
    n = w.shape[1]
    tn = TN_DENSE
    assert m % tm == 0 and n % tn == 0
    return pl.pallas_call(
        _matmul_res_kernel,
        grid=(m // tm, n // tn),
        in_specs=[pl.BlockSpec((tm, k), lambda i, j: (i, 0)),
                  pl.BlockSpec((k, tn), lambda i, j: (0, j)),
                  pl.BlockSpec((tm, tn), lambda i, j: (i, j))],
        out_specs=pl.BlockSpec((tm, tn), lambda i, j: (i, j)),
        out_shape=jax.ShapeDtypeStruct((m, n), F32),
        compiler_params=_params("parallel", "arbitrary"),
    )(a, w, res)


def _merge_kernel(oa_ref, ob_ref, oc_ref, od_ref, wb_ref, ga_ref, gb_ref, gc_ref, gd_ref, o_ref):
    acc = None
    for n, (o, g) in enumerate(((oa_ref, ga_ref), (ob_ref, gb_ref), (oc_ref, gc_ref), (od_ref, gd_ref))):
        t = jax.nn.sigmoid(g[...]) * _dot(o[...], wb_ref[n])
        acc = t if acc is None else acc + t
    o_ref[...] = acc.astype(o_ref.dtype)


def _merge(branches, w_branch, gate_logits):
    m = branches[0].shape[0]
    tm, tn = min(TM_DENSE, m), TN_DENSE
    nj = D_MODEL // tn
    bspec = pl.BlockSpec((tm, BRANCH_W), lambda i, j: (i, 0))
    gspecs = [pl.BlockSpec((tm, tn), functools.partial(lambda i, j, n: (i, n * nj + j), n=n))
              for n in range(N_BRANCH)]
    return pl.pallas_call(
        _merge_kernel,
        grid=(m // tm, nj),
        in_specs=[bspec] * N_BRANCH + [pl.BlockSpec((N_BRANCH, BRANCH_W, tn), lambda i, j: (0, 0, j))] + gspecs,
        out_specs=pl.BlockSpec((tm, tn), lambda i, j: (i, j)),
        out_shape=jax.ShapeDtypeStruct((m, D_MODEL), BF16),
        compiler_params=_params("parallel", "arbitrary"),
    )(*branches, w_branch, gate_logits, gate_logits, gate_logits, gate_logits)


def _rope64(x, cos, sin_signed):
    w = x.shape[-1]
    lane = lax.broadcasted_iota(jnp.int32, x.shape, x.ndim - 1)
    first_half = (lane % D_ROPE) < (D_ROPE // 2)
    partner = jnp.where(first_half, pltpu.roll(x, w - D_ROPE // 2, x.ndim - 1), pltpu.roll(x, D_ROPE // 2, x.ndim - 1))
    return x * cos + partner * sin_signed


def _rope_tables(pos, width):
    half = D_ROPE // 2
    inv = ROPE_THETA ** (-jnp.arange(half, dtype=F32) / half)
    ang = pos.astype(F32)[:, None] * inv[None, :]
    cos, sin = jnp.cos(ang), jnp.sin(ang)
    reps = width // D_ROPE
    return (jnp.tile(jnp.concatenate([cos, cos], axis=-1), (1, reps)),
            jnp.tile(jnp.concatenate([-sin, sin], axis=-1), (1, reps)))


def _mla_prep_kernel(z_ref, gq_ref, wn_ref, wr_ref, ghn_ref, ghr_ref, wuk_ref, gkv_ref, gkr_ref, cos_ref, sin_ref,
                     qcat_ref, kcat_ref, ckv_ref, krope_ref):
    z = z_ref[...]
    qn = _rms(z[:, :Q_LORA], gq_ref[...]).astype(BF16)
    qr = _dot(qn, wr_ref[...])
    lane = lax.broadcasted_iota(jnp.int32, qr.shape, 1)
    qr2 = qr * qr
    scale_lanes = jnp.zeros_like(qr)
    q_lat = []
    for h in range(H_A):
        in_head = (lane // D_ROPE) == h
        qh = _dot(qn, wn_ref[h])
        ss = jnp.sum(qh * qh, axis=-1, keepdims=True) + jnp.sum(jnp.where(in_head, qr2, 0.0), axis=-1, keepdims=True)
        r = lax.rsqrt(ss * (1.0 / (D_NOPE + D_ROPE)) + EPS)
        scale_lanes = jnp.where(in_head, r, scale_lanes)
        q_lat.append(_dot((qh * r * ghn_ref[...]).astype(BF16), wuk_ref[h]))
    cos, sin = cos_ref[...], sin_ref[...]
    qrr = _rope64(qr * scale_lanes * ghr_ref[...], cos, sin)
    for h in range(H_A):
        in_head = (lane // D_ROPE) == h
        qcat_ref[h] = jnp.concatenate([q_lat[h] * MLA_QSCALE, jnp.where(in_head, qrr, 0.0) * MLA_QSCALE],
                                      axis=-1).astype(BF16)
    ckv = _rms(z[:, Q_LORA:Q_LORA + KV_LORA], gkv_ref[...])
    krr = _rope64(_rms(z[:, Q_LORA + KV_LORA:], gkr_ref[...]), cos, sin)
    ckv_ref[...] = ckv
    krope_ref[...] = krr[:, :D_ROPE]
    kcat_ref[...] = jnp.concatenate([ckv, krr], axis=-1).astype(BF16)


def _mla_prep(z_a, p, cos, sin):
    m = z_a.shape[0]
    tm = TM_PREP
    wa = z_a.shape[1]
    wr = H_A * D_ROPE
    const = lambda *shape: pl.BlockSpec(shape, lambda i: (0,) * len(shape))
    return pl.pallas_call(
        _mla_prep_kernel,
        grid=(m // tm,),
        in_specs=[pl.BlockSpec((tm, wa), lambda i: (i, 0)),
                  const(1, Q_LORA), const(H_A, Q_LORA, D_NOPE), const(Q_LORA, wr), const(1, D_NOPE), const(1, wr),
                  const(H_A, D_NOPE, KV_LORA), const(1, KV_LORA), const(1, wr),
                  pl.BlockSpec((tm, wr), lambda i: (i, 0)), pl.BlockSpec((tm, wr), lambda i: (i, 0))],
        out_specs=[pl.BlockSpec((H_A, tm, KV_LORA + wr), lambda i: (0, i, 0)),
                   pl.BlockSpec((tm, KV_LORA + wr), lambda i: (i, 0)),
                   pl.BlockSpec((tm, KV_LORA), lambda i: (i, 0)),
                   pl.BlockSpec((tm, D_ROPE), lambda i: (i, 0))],
        out_shape=[jax.ShapeDtypeStruct((H_A, m, KV_LORA + wr), BF16),
                   jax.ShapeDtypeStruct((m, KV_LORA + wr), BF16),
                   jax.ShapeDtypeStruct((m, KV_LORA), F32),
                   jax.ShapeDtypeStruct((m, D_ROPE), F32)],
        compiler_params=_params("parallel"),
    )(z_a, p["g_q"], p["w_nope"], p["w_rope"], p["g_hn"], p["g_hr"], p["w_ukT"], p["g_kv"], p["g_kr"], cos, sin)


def _group_rms64(x, gain_tiled):
    lane = lax.broadcasted_iota(jnp.int32, x.shape, 1)
    x2 = x * x
    scale = jnp.zeros_like(x)
    for g in range(x.shape[1] // DK_B):
        in_g = (lane // DK_B) == g
        ss = jnp.sum(jnp.where(in_g, x2, 0.0), axis=-1, keepdims=True)
        scale = jnp.where(in_g, lax.rsqrt(ss * (1.0 / DK_B) + EPS), scale)
    return x * scale * gain_tiled


def _diff_prep_kernel(z_ref, gq_ref, gk_ref, q_ref, k_ref, v_ref):
    z = z_ref[...]
    nq = H_B * 2 * DK_B
    q_ref[...] = (_group_rms64(z[:, :nq], gq_ref[...]) * DIFF_QSCALE).astype(BF16)
    k_ref[...] = _group_rms64(z[:, nq:nq + 2 * DK_B], gk_ref[...])
    v_ref[...] = z[:, nq + 2 * DK_B:]


def _diff_prep(z_b, gq_tiled, gk_tiled):
    m, wb = z_b.shape
    tm = TM_PREP
    nq = H_B * 2 * DK_B
    return pl.pallas_call(
        _diff_prep_kernel,
        grid=(m // tm,),
        in_specs=[pl.BlockSpec((tm, wb), lambda i: (i, 0)),
                  pl.BlockSpec((1, nq), lambda i: (0, 0)), pl.BlockSpec((1, 2 * DK_B), lambda i: (0, 0))],
        out_specs=[pl.BlockSpec((tm, nq), lambda i: (i, 0)),
                   pl.BlockSpec((tm, 2 * DK_B), lambda i: (i, 0)),
                   pl.BlockSpec((tm, DV_B), lambda i: (i, 0))],
        out_shape=[jax.ShapeDtypeStruct((m, nq), BF16),
                   jax.ShapeDtypeStruct((m, 2 * DK_B), F32),
                   jax.ShapeDtypeStruct((m, DV_B), F32)],
        compiler_params=_params("parallel"),
    )(z_b, gq_tiled, gk_tiled)


def _widen(x, width):
    return x if width == LANES else jnp.concatenate([x] * (width // LANES), axis=1)


def _softmax_step(s, m_prev, l_prev, shift=None):
    m_cur = jnp.max(s, axis=-1, keepdims=True)
    if shift is not None:
        m_cur = m_cur + shift
    m_next = jnp.maximum(m_prev, m_cur)
    alpha = jnp.exp2(m_prev - m_next)
    p = jnp.exp2(s - _widen(m_next if shift is None else m_next - shift, s.shape[1]))
    l_next = alpha * l_prev + jnp.sum(p, axis=-1, keepdims=True)
    return p, m_next, l_next, alpha


def _init_softmax_state(m_ref, l_ref, acc_ref):
    m_ref[...] = jnp.full(m_ref.shape, NEG_INF, F32)
    l_ref[...] = jnp.zeros(l_ref.shape, F32)
    acc_ref[...] = jnp.zeros(acc_ref.shape, F32)


def _causal_tiles(n):
    pairs = [(i, j) for i in range(n) for j in range(i + 1)]
    return jnp.asarray([p[0] for p in pairs], jnp.int32), jnp.asarray([p[1] for p in pairs], jnp.int32)


def _mla_prompt_kernel(qi_ref, kj_ref, q_ref, k_ref, wuv_ref, o_ref, m_ref, l_ref, acc_ref):
    step = pl.program_id(0)
    i, j = qi_ref[step], kj_ref[step]
    t = k_ref.shape[0]

    @pl.when(j == 0)
    def _():
        _init_softmax_state(m_ref, l_ref, acc_ref)

    def tile(diagonal):
        k = k_ref[...]
        v = k[:, :KV_LORA]
        if diagonal:
            visible = lax.broadcasted_iota(jnp.int32, (t, t), 1) <= lax.broadcasted_iota(jnp.int32, (t, t), 0)
        for h in range(H_A):
            s = _dot_nt(q_ref[h], k)
            if diagonal:
                s = jnp.where(visible, s, NEG_INF)
            p, m_next, l_next, alpha = _softmax_step(s, m_ref[h], l_ref[h])
            m_ref[h] = m_next
            l_ref[h] = l_next
            acc_ref[h] = acc_ref[h] * _widen(alpha, KV_LORA) + _dot(p.astype(BF16), v)

    @pl.when(j < i)
    def _():
        tile(False)

    @pl.when(j == i)
    def _():
        tile(True)
        for h in range(H_A):
            o_lat = acc_ref[h] / _widen(l_ref[h], KV_LORA)
            o_ref[:, h * DV_A:(h + 1) * DV_A] = _dot(o_lat.astype(BF16), wuv_ref[h]).astype(o_ref.dtype)


def _mla_prompt(qcat, kcat, w_uv, t_len):
    t = T_ATTN
    qi, kj = _causal_tiles(t_len // t)
    dk = kcat.shape[1]
    return pl.pallas_call(
        _mla_prompt_kernel,
        grid_spec=pltpu.PrefetchScalarGridSpec(
            num_scalar_prefetch=2,
            grid=(qi.shape[0],),
            in_specs=[pl.BlockSpec((H_A, t, dk), lambda s, qi, kj: (0, qi[s], 0)),
                      pl.BlockSpec((t, dk), lambda s, qi, kj: (kj[s], 0)),
                      pl.BlockSpec((H_A, KV_LORA, DV_A), lambda s, qi, kj: (0, 0, 0))],
            out_specs=pl.BlockSpec((t, BRANCH_W), lambda s, qi, kj: (qi[s], 0)),
            scratch_shapes=[pltpu.VMEM((H_A, t, LANES), F32), pltpu.VMEM((H_A, t, LANES), F32),
                            pltpu.VMEM((H_A, t, KV_LORA), F32)]),
        out_shape=jax.ShapeDtypeStruct((t_len, BRANCH_W), BF16),
        compiler_params=_params("arbitrary"),
    )(qi, kj, qcat, kcat, w_uv)


def _diff_finish(acc0, l0, acc1, l1, lam, gain, out_scale):
    o = acc0 / l0 - lam * (acc1 / l1)
    return _rms(o, gain) * out_scale


def _diff_prompt_kernel(qi_ref, kj_ref, lam_ref, far_ref, q_ref, k_ref, v_ref, bias_ref, g_ref, o_ref,
                        m_ref, l_ref, acc_ref, *, out_scale):
    step = pl.program_id(0)
    i, j = qi_ref[step], kj_ref[step]
    t = k_ref.shape[0]

    @pl.when(j == 0)
    def _():
        _init_softmax_state(m_ref, l_ref, acc_ref)

    def tile(kind):
        k = k_ref[...]
        v = v_ref[...]
        lane = lax.broadcasted_iota(jnp.int32, (t, 2 * DK_B), 1)
        if kind == "diag":
            visible = lax.broadcasted_iota(jnp.int32, (t, t), 1) <= lax.broadcasted_iota(jnp.int32, (t, t), 0)
        for h in range(H_B):
            qh = q_ref[:, h * 2 * DK_B:(h + 1) * 2 * DK_B]
            for mp in range(2):
                qm = jnp.where((lane // DK_B) == mp, qh, jnp.zeros_like(qh))
                s = _dot_nt(qm, k)
                shift = None
                if kind == "far":
                    shift = far_ref[h]
                else:
                    s = s + bias_ref[0 if kind == "near" else 1, h]
                if kind == "diag":
                    s = jnp.where(visible, s, NEG_INF)
                idx = mp * H_B + h
                p, m_next, l_next, alpha = _softmax_step(s, m_ref[idx], l_ref[idx], shift)
                m_ref[idx] = m_next
                l_ref[idx] = l_next
                acc_ref[idx] = acc_ref[idx] * alpha + _dot(p.astype(BF16), v)

    @pl.when(j < i - 1)
    def _():
        tile("far")

    @pl.when(j == i - 1)
    def _():
        tile("near")

    @pl.when(j == i)
    def _():
        tile("diag")
        lam = lam_ref[0]
        for h in range(H_B):
            o = _diff_finish(acc_ref[h], l_ref[h], acc_ref[H_B + h], l_ref[H_B + h], lam, g_ref[...], out_scale)
            o_ref[:, h * DV_B:(h + 1) * DV_B] = o.astype(o_ref.dtype)


def _diff_prompt(qb, kb, vb, bias_patches, bias_far, lam, g_out, out_scale, t_len):
    t = T_ATTN
    assert t >= T5_MAX_DIST
    qi, kj = _causal_tiles(t_len // t)
    return pl.pallas_call(
        functools.partial(_diff_prompt_kernel, out_scale=out_scale),
        grid_spec=pltpu.PrefetchScalarGridSpec(
            num_scalar_prefetch=4,
            grid=(qi.shape[0],),
            in_specs=[pl.BlockSpec((t, H_B * 2 * DK_B), lambda s, qi, kj, lam, far: (qi[s], 0)),
                      pl.BlockSpec((t, 2 * DK_B), lambda s, qi, kj, lam, far: (kj[s], 0)),
                      pl.BlockSpec((t, DV_B), lambda s, qi, kj, lam, far: (kj[s], 0)),
                      pl.BlockSpec((2, H_B, t, t), lambda s, qi, kj, lam, far: (0, 0, 0, 0)),
                      pl.BlockSpec((1, DV_B), lambda s, qi, kj, lam, far: (0, 0))],
            out_specs=pl.BlockSpec((t, BRANCH_W), lambda s, qi, kj, lam, far: (qi[s], 0)),
            scratch_shapes=[pltpu.VMEM((2 * H_B, t, LANES), F32), pltpu.VMEM((2 * H_B, t, LANES), F32),
                            pltpu.VMEM((2 * H_B, t, DV_B), F32)]),
        out_shape=jax.ShapeDtypeStruct((t_len, BRANCH_W), BF16),
        compiler_params=_params("arbitrary"),
    )(qi, kj, lam, bias_far, qb, kb, vb, bias_patches, g_out)


def _new_token_mask(rows, t_new):
    row = lax.broadcasted_iota(jnp.int32, (rows, PAGE_SIZE), 0)
    col = lax.broadcasted_iota(jnp.int32, (rows, PAGE_SIZE), 1)
    return col <= (row % t_new)


def _softmax_pv(s_ref, v_ref, n_total):
    rows = s_ref.shape[1]
    unroll = PAGES_PER_STEP

    def page_max(n, m):
        return jnp.maximum(m, s_ref[n])

    m = lax.fori_loop(0, n_total, page_max, jnp.full((rows, PAGE_SIZE), NEG_INF, F32), unroll=unroll)
    m = jnp.max(m, axis=-1, keepdims=True)

    def page_pv(n, carry):
        l, acc = carry
        p = jnp.exp2(s_ref[n] - m)
        return l + p, acc + _dot(p.astype(BF16), v_ref[n])

    l, acc = lax.fori_loop(0, n_total, page_pv,
                           (jnp.zeros((rows, PAGE_SIZE), F32), jnp.zeros((rows, v_ref.shape[2]), F32)), unroll=unroll)
    return acc / jnp.sum(l, axis=-1, keepdims=True)


def _page_gather(pt_ref, hbm_a, hbm_b, buf_a, buf_b, sem, li, n_pages):
    b, j = pl.program_id(0), pl.program_id(1)
    nj = pl.num_programs(1)
    step = b * nj + j
    slot = step % 2

    def copies(bb, jj, sl):
        out = []
        for n in range(n_pages):
            page = pt_ref[bb, jj * n_pages + n]
            out.append(pltpu.make_async_copy(hbm_a.at[li, page], buf_a.at[sl, n], sem.at[0, sl]))
            out.append(pltpu.make_async_copy(hbm_b.at[li, page], buf_b.at[sl, n], sem.at[1, sl]))
        return out

    @pl.when(step == 0)
    def _():
        for c in copies(b, j, slot):
            c.start()

    @pl.when(step + 1 < pl.num_programs(0) * nj)
    def _():
        wrap = j == nj - 1
        for c in copies(jnp.where(wrap, b + 1, b), jnp.where(wrap, 0, j + 1), 1 - slot):
            c.start()

    for c in copies(b, j, slot):
        c.wait()
    return slot


def _mla_sample_kernel(pt_ref, ql_ref, qr_ref, nl_ref, nr_ref, wuv_ref, lat_hbm, rope_hbm, o_ref,
                       s_ref, v_ref, lat_buf, rope_buf, sem, *, n_pages, t_new, li):
    j = pl.program_id(1)
    ql, qr = ql_ref[0], qr_ref[0]
    n_past = n_pages * pl.num_programs(1)
    slot = _page_gather(pt_ref, lat_hbm, rope_hbm, lat_buf, rope_buf, sem, li, n_pages)
    for n in range(n_pages):
        lat = lat_buf[slot, n].astype(BF16)
        v_ref[j * n_pages + n] = lat
        s_ref[j * n_pages + n] = _dot_nt(ql, lat) + _dot(qr, rope_buf[slot, n].astype(BF16))

    @pl.when(j == pl.num_programs(1) - 1)
    def _():
        lat = nl_ref[0].astype(BF16)
        v_ref[n_past] = lat
        s_new = _dot_nt(ql, lat) + _dot_nt(qr, nr_ref[0].astype(BF16))
        s_ref[n_past] = jnp.where(_new_token_mask(ql.shape[0], t_new), s_new, NEG_INF)
        o_lat = _softmax_pv(s_ref, v_ref, n_past + 1)
        for h in range(H_A):
            o_ref[0, :, h * DV_A:(h + 1) * DV_A] = _dot(o_lat[h * t_new:(h + 1) * t_new].astype(BF16),
                                                      wuv_ref[h]).astype(o_ref.dtype)


def _mla_sample(page_table, q_lat, q_rope, new_lat, new_rope, w_uv, cache_lat, cache_rope_t, li):
    b, rows, _ = q_lat.shape
    t_new = rows // H_A
    n_pages = PAGES_PER_STEP
    total = page_table.shape[1]
    steps = total // n_pages
    per_seq = lambda r, w: pl.BlockSpec((1, r, w), lambda bi, j, pt: (bi, 0, 0))
    hbm = pl.BlockSpec(memory_space=pl.ANY)
    return pl.pallas_call(
        functools.partial(_mla_sample_kernel, n_pages=n_pages, t_new=t_new, li=li),
        grid_spec=pltpu.PrefetchScalarGridSpec(
            num_scalar_prefetch=1,
            grid=(b, steps),
            in_specs=[per_seq(rows, KV_LORA), per_seq(rows, D_ROPE), per_seq(PAGE_SIZE, KV_LORA),
                      per_seq(PAGE_SIZE, D_ROPE),
                      pl.BlockSpec((H_A, KV_LORA, DV_A), lambda bi, j, pt: (0, 0, 0)), hbm, hbm],
            out_specs=pl.BlockSpec((1, t_new, BRANCH_W), lambda bi, j, pt: (bi, 0, 0)),
            scratch_shapes=[pltpu.VMEM((total + 1, rows, PAGE_SIZE), F32),
                            pltpu.VMEM((total + 1, PAGE_SIZE, KV_LORA), BF16),
                            pltpu.VMEM((2, n_pages, PAGE_SIZE, KV_LORA), F32),
                            pltpu.VMEM((2, n_pages, D_ROPE, PAGE_SIZE), F32),
                            pltpu.SemaphoreType.DMA((2, 2))]),
        out_shape=jax.ShapeDtypeStruct((b, t_new, BRANCH_W), F32),
        compiler_params=_params("arbitrary", "arbitrary"),
    )(page_table, q_lat, q_rope, new_lat, new_rope, w_uv, cache_lat, cache_rope_t)


def _diff_sample_kernel(pt_ref, lam_ref, q_ref, nk_ref, nv_ref, bias_ref, nbias_ref, g_ref, k_hbm, v_hbm, o_ref,
                        s_ref, vs_ref, k_buf, v_buf, sem, *, n_pages, t_new, out_scale, li):
    j = pl.program_id(1)
    q = q_ref[0]
    n_past = n_pages * pl.num_programs(1)
    slot = _page_gather(pt_ref, k_hbm, v_hbm, k_buf, v_buf, sem, li, n_pages)
    for n in range(n_pages):
        bias = bias_ref[:, n * PAGE_SIZE:(n + 1) * PAGE_SIZE]
        vs_ref[j * n_pages + n] = v_buf[slot, n].astype(BF16)
        s_ref[j * n_pages + n] = _dot_nt(q, k_buf[slot, n].astype(BF16)) + jnp.concatenate([bias, bias], axis=0)

    @pl.when(j == pl.num_programs(1) - 1)
    def _():
        nbias = nbias_ref[...]
        vs_ref[n_past] = nv_ref[0].astype(BF16)
        s_new = _dot_nt(q, nk_ref[0].astype(BF16)) + jnp.concatenate([nbias, nbias], axis=0)
        s_ref[n_past] = jnp.where(_new_token_mask(q.shape[0], t_new), s_new, NEG_INF)
        a = _softmax_pv(s_ref, vs_ref, n_past + 1)
        half = H_B * t_new
        o = _rms(a[:half] - lam_ref[0] * a[half:], g_ref[...]) * out_scale
        for h in range(H_B):
            o_ref[0, :, h * DV_B:(h + 1) * DV_B] = o[h * t_new:(h + 1) * t_new].astype(o_ref.dtype)


def _diff_sample(page_table, lam, q_pad, new_k, new_v, bias, new_bias, g_out, out_scale, cache_k, cache_v, li):
    b, rows, _ = q_pad.shape
    t_new = rows // (2 * H_B)
    n_pages = PAGES_PER_STEP
    total = page_table.shape[1]
    steps = total // n_pages
    per_seq = lambda r, w: pl.BlockSpec((1, r, w), lambda bi, j, pt, lm: (bi, 0, 0))
    hbm = pl.BlockSpec(memory_space=pl.ANY)
    return pl.pallas_call(
        functools.partial(_diff_sample_kernel, n_pages=n_pages, t_new=t_new, out_scale=out_scale, li=li),
        grid_spec=pltpu.PrefetchScalarGridSpec(
            num_scalar_prefetch=2,
            grid=(b, steps),
            in_specs=[per_seq(rows, 2 * DK_B), per_seq(PAGE_SIZE, 2 * DK_B), per_seq(PAGE_SIZE, DV_B),
                      pl.BlockSpec((rows // 2, n_pages * PAGE_SIZE), lambda bi, j, pt, lm: (0, j)),
                      pl.BlockSpec((rows // 2, PAGE_SIZE), lambda bi, j, pt, lm: (0, 0)),
                      pl.BlockSpec((1, DV_B), lambda bi, j, pt, lm: (0, 0)), hbm, hbm],
            out_specs=pl.BlockSpec((1, t_new, BRANCH_W), lambda bi, j, pt, lm: (bi, 0, 0)),
            scratch_shapes=[pltpu.VMEM((total + 1, rows, PAGE_SIZE), F32),
                            pltpu.VMEM((total + 1, PAGE_SIZE, DV_B), BF16),
                            pltpu.VMEM((2, n_pages, PAGE_SIZE, 2 * DK_B), F32),
                            pltpu.VMEM((2, n_pages, PAGE_SIZE, DV_B), F32),
                            pltpu.SemaphoreType.DMA((2, 2))]),
        out_shape=jax.ShapeDtypeStruct((b, t_new, BRANCH_W), F32),
        compiler_params=_params("arbitrary", "arbitrary"),
    )(page_table, lam, q_pad, new_k, new_v, bias, new_bias, g_out, cache_k, cache_v)


def _cumsum_rows(x):
    n = x.shape[0]
    row = lax.broadcasted_iota(jnp.int32, x.shape, 0)
    sh = 1
    while sh < n:
        x = x + jnp.where(row >= sh, pltpu.roll(x, sh, 0), 0.0)
        sh *= 2
    return x


def _group_row(x, group, j):
    n, w = x.shape
    if group == n:
        return jnp.broadcast_to(x[j:j + 1, :], (n, w))
    x3 = x.reshape(n // group, group, w)
    return jnp.broadcast_to(x3[:, j:j + 1, :], x3.shape).reshape(n, w)


def _hgrn_gates(cq, cf, a_lb, c_lb, oml):
    q = cq * jax.nn.sigmoid(cq)
    log_sig = jnp.minimum(cf, 0.0) - jnp.log1p(jnp.exp(-jnp.abs(cf)))
    x = c_lb + log_sig
    g = jnp.maximum(a_lb, x) + jnp.log1p(jnp.exp(-jnp.abs(a_lb - x)))
    k = oml * jax.nn.sigmoid(-cf)
    return q, k, g


def _hgrn_tile(q, k, v, g, st, sub):
    n = q.shape[0]
    b = _cumsum_rows(g)
    row = lax.broadcasted_iota(jnp.int32, (n, 1), 0)
    o = _dot_nt((q * jnp.exp(b)).astype(BF16), st.astype(BF16))
    for j in range(sub):
        e = jnp.where((row % sub) >= j, b - _group_row(b, sub, j), NEG_INF)
        r = jnp.sum(q * _group_row(k, sub, j) * jnp.exp(e), axis=-1, keepdims=True)
        o = o + r * _group_row(v, sub, j)
    half = sub
    while half < n:
        upper = (row % (2 * half)) >= half
        bb = _group_row(b, 2 * half, half - 1)
        qe = q * jnp.exp(jnp.where(upper, b - bb, NEG_INF))
        ke = k * jnp.exp(jnp.where(upper, NEG_INF, bb - b))
        s = _dot_nt(qe.astype(BF16), ke.astype(BF16))
        trow = lax.broadcasted_iota(jnp.int32, (n, n), 0) // (2 * half)
        tcol = lax.broadcasted_iota(jnp.int32, (n, n), 1) // (2 * half)
        o = o + _dot(jnp.where(trow == tcol, s, 0.0).astype(BF16), v.astype(BF16))
        half *= 2
    b_last = b[n - 1:n, :]
    kd = k * jnp.exp(b_last - b)
    st_next = st * jnp.exp(b_last) + _dot_tn(v.astype(BF16), kd.astype(BF16))
    return o, st_next


def _hgrn_prompt_kernel(cq_ref, cf_ref, ci_ref, cg_ref, a_ref, c_ref, oml_ref, go_ref, o_ref, s_ref, st_ref):
    t = pl.program_id(0)

    @pl.when(t == 0)
    def _():
        st_ref[...] = jnp.zeros(st_ref.shape, F32)

    for h in range(H_C):
        hl = slice(h * DK_C, (h + 1) * DK_C)
        q, k, g = _hgrn_gates(cq_ref[:, hl], cf_ref[:, hl], a_ref[:, hl], c_ref[:, hl], oml_ref[:, hl])
        o, st_next = _hgrn_tile(q, k, ci_ref[:, hl], g, st_ref[h], SUB)
        st_ref[h] = st_next
        o_ref[:, hl] = (_rms(o, go_ref[...]) * jax.nn.sigmoid(cg_ref[:, hl])).astype(o_ref.dtype)

        @pl.when(t == pl.num_programs(0) - 1)
        def _():
            s_ref[h] = st_next.T


def _hgrn_prompt(z_c, a_lb, c_lb, oml, g_out, t_len):
    tt = T_REC
    w = H_C * DK_C
    sect = lambda s: pl.BlockSpec((tt, w), functools.partial(lambda t, s: (t, s), s=s))
    vec = pl.BlockSpec((1, w), lambda t: (0, 0))
    return pl.pallas_call(
        _hgrn_prompt_kernel,
        grid=(t_len // tt,),
        in_specs=[sect(0), sect(1), sect(2), sect(3), vec, vec, vec, pl.BlockSpec((1, DV_C), lambda t: (0, 0))],
        out_specs=[pl.BlockSpec((tt, BRANCH_W), lambda t: (t, 0)),
                   pl.BlockSpec((H_C, DK_C, DV_C), lambda t: (0, 0, 0))],
        out_shape=[jax.ShapeDtypeStruct((t_len, BRANCH_W), BF16),
                   jax.ShapeDtypeStruct((H_C, DK_C, DV_C), F32)],
        scratch_shapes=[pltpu.VMEM((H_C, DV_C, DK_C), F32)],
        compiler_params=_params("arbitrary"),
    )(z_c, z_c, z_c, z_c, a_lb, c_lb, oml, g_out)


def _pad_rows(x, n):
    return jnp.concatenate([x, jnp.zeros((n - x.shape[0], x.shape[1]), x.dtype)], axis=0)


def _hgrn_sample_kernel(z_ref, s0_ref, a_ref, c_ref, oml_ref, go_ref, o_ref, s_ref):
    n = z_ref.shape[1]
    w = H_C * DK_C
    row = lax.broadcasted_iota(jnp.int32, (n, 1), 0)
    for sq in range(z_ref.shape[0]):
        z = z_ref[sq]
        for h in range(H_C):
            lanes = lambda s: slice(s * w + h * DK_C, s * w + (h + 1) * DK_C)
            hl = slice(h * DK_C, (h + 1) * DK_C)
            q, k, g = _hgrn_gates(z[:, lanes(0)], z[:, lanes(1)], a_ref[:, hl], c_ref[:, hl], oml_ref[:, hl])
            v = z[:, lanes(2)]
            s0 = s0_ref[0, sq, h]
            b = _cumsum_rows(g)
            o = _dot((q * jnp.exp(b)).astype(BF16), s0.astype(BF16))
            for j in range(n):
                e = jnp.where(row >= j, b - b[j:j + 1, :], NEG_INF)
                r = jnp.sum(q * k[j:j + 1, :] * jnp.exp(e), axis=-1, keepdims=True)
                o = o + r * v[j:j + 1, :]
            b_last = b[n - 1:n, :]
            kd = _pad_rows((k * jnp.exp(b_last - b)).astype(BF16), LANES)
            st_next = s0.T * jnp.exp(b_last) + _dot_tn(_pad_rows(v.astype(BF16), LANES), kd)
            s_ref[sq, h] = st_next.T
            o_ref[sq, :, hl] = (_rms(o, go_ref[...]) * jax.nn.sigmoid(z[:, lanes(3)])).astype(o_ref.dtype)


def _hgrn_sample(z_c, state, li, a_lb, c_lb, oml, g_out):
    b, t_new, w = z_c.shape
    ns = SEQS_PER_STEP
    assert b % ns == 0
    vec = pl.BlockSpec((1, H_C * DK_C), lambda i: (0, 0))
    return pl.pallas_call(
        _hgrn_sample_kernel,
        grid=(b // ns,),
        in_specs=[pl.BlockSpec((ns, t_new, w), lambda i: (i, 0, 0)),
                  pl.BlockSpec((1, ns, H_C, DK_C, DV_C), lambda i: (li, i, 0, 0, 0)),
                  vec, vec, vec, pl.BlockSpec((1, DV_C), lambda i: (0, 0))],
        out_specs=[pl.BlockSpec((ns, t_new, BRANCH_W), lambda i: (i, 0, 0)),
                   pl.BlockSpec((ns, H_C, DK_C, DV_C), lambda i: (i, 0, 0, 0))],
        out_shape=[jax.ShapeDtypeStruct((b, t_new, BRANCH_W), F32),
                   jax.ShapeDtypeStruct((b, H_C, DK_C, DV_C), F32)],
        compiler_params=_params("parallel"),
    )(z_c, state, a_lb, c_lb, oml, g_out)


def _ret_log_decay():
    return np.log1p(-np.exp2(-5.0 - np.arange(H_D, dtype=np.float64)))


def _ret_tables(n, block):
    lg = _ret_log_decay()
    t = np.arange(n)
    loc = t % block
    same = (t[:, None] // block) == (t[None, :] // block)
    diff = loc[:, None] - loc[None, :]
    dmat = np.where(same & (diff >= 0), np.exp(lg[:, None, None] * np.maximum(diff, 0)[None]), 0.0)
    per_lane = lambda f: np.repeat(f, DK_D, axis=1)
    dq = per_lane(np.exp(lg[None, :] * (loc[:, None] + 1)))
    dk = per_lane(np.exp(lg[None, :] * (block - 1 - loc[:, None])))
    dend = per_lane(np.exp(lg[None, :] * block))
    return tuple(jnp.asarray(a, F32) for a in (dmat, dq, dk, dend))


def _ret_prompt_kernel(rq_ref, rk_ref, rv_ref, rg_ref, cos_ref, sin_ref, dmat_ref, dq_ref, dk_ref, dend_ref, go_ref,
                       o_ref, s_ref, st_ref):
    t = pl.program_id(0)

    @pl.when(t == 0)
    def _():
        st_ref[...] = jnp.zeros(st_ref.shape, F32)

    cos, sin = cos_ref[...], sin_ref[...]
    q = _rope64(rq_ref[...], cos, sin)
    k = _rope64(rk_ref[...], cos, sin) * (DK_D ** -0.5)
    lane = lax.broadcasted_iota(jnp.int32, q.shape, 1)
    k_bf = k.astype(BF16)
    qd = q * dq_ref[...]
    kd = k * dk_ref[...]
    for h in range(H_D):
        in_head = (lane // DK_D) == h
        vl = slice(h * DV_D, (h + 1) * DV_D)
        v = rv_ref[:, vl].astype(BF16)
        st = st_ref[h]
        s = _dot_nt(jnp.where(in_head, q, 0.0).astype(BF16), k_bf) * dmat_ref[h]
        o = _dot(s.astype(BF16), v) + _dot_nt(jnp.where(in_head, qd, 0.0).astype(BF16), st.astype(BF16))
        st_next = st * dend_ref[...] + _dot_tn(v, jnp.where(in_head, kd, 0.0).astype(BF16))
        st_ref[h] = st_next
        o_ref[:, vl] = (_rms(o, go_ref[...]) * (rg_ref[:, vl] * jax.nn.sigmoid(rg_ref[:, vl]))).astype(o_ref.dtype)

        @pl.when(t == pl.num_programs(0) - 1)
        def _():
            s_ref[h] = st_next.T[h * DK_D:(h + 1) * DK_D, :]


def _ret_prompt(z_d, cos, sin, g_out, t_len):
    tt = T_RET
    wq = H_D * DK_D
    wv = H_D * DV_D
    dmat, dq, dk, dend = _ret_tables(tt, tt)
    const = lambda *shape: pl.BlockSpec(shape, lambda t: (0,) * len(shape))
    return pl.pallas_call(
        _ret_prompt_kernel,
        grid=(t_len // tt,),
        in_specs=[pl.BlockSpec((tt, wq), lambda t: (t, 0)), pl.BlockSpec((tt, wq), lambda t: (t, 1)),
                  pl.BlockSpec((tt, wv), lambda t: (t, 1)), pl.BlockSpec((tt, wv), lambda t: (t, 2)),
                  pl.BlockSpec((tt, wq), lambda t: (t, 0)), pl.BlockSpec((tt, wq), lambda t: (t, 0)),
                  const(H_D, tt, tt), const(tt, wq), const(tt, wq), const(1, wq), const(1, DV_D)],
        out_specs=[pl.BlockSpec((tt, BRANCH_W), lambda t: (t, 0)),
                   pl.BlockSpec((H_D, DK_D, DV_D), lambda t: (0, 0, 0))],
        out_shape=[jax.ShapeDtypeStruct((t_len, BRANCH_W), BF16),
                   jax.ShapeDtypeStruct((H_D, DK_D, DV_D), F32)],
        scratch_shapes=[pltpu.VMEM((H_D, DV_D, wq), F32)],
        compiler_params=_params("arbitrary"),
    )(z_d, z_d, z_d, z_d, cos, sin, dmat, dq, dk, dend, g_out)


def _ret_sample_kernel(z_ref, s0_ref, cos_ref, sin_ref, dmat_ref, dq_ref, dk_ref, dend_ref, go_ref, o_ref, s_ref):
    wq = H_D * DK_D
    cos, sin = cos_ref[...], sin_ref[...]
    for sq in range(z_ref.shape[0]):
        z = z_ref[sq]
        q = _rope64(z[:, :wq], cos, sin)
        k = _rope64(z[:, wq:2 * wq], cos, sin) * (DK_D ** -0.5)
        lane = lax.broadcasted_iota(jnp.int32, q.shape, 1)
        k_pad = _pad_rows(k.astype(BF16), LANES)
        qd = q * dq_ref[...]
        kd = _pad_rows((k * dk_ref[...]).astype(BF16), LANES)
        lane_pad = lax.broadcasted_iota(jnp.int32, kd.shape, 1)
        s_all = s0_ref[0, sq].reshape(wq, DV_D)
        for h in range(H_D):
            in_head = (lane // DK_D) == h
            rows = slice(h * DK_D, (h + 1) * DK_D)
            vl = slice(2 * wq + h * DV_D, 2 * wq + (h + 1) * DV_D)
            gl = slice(2 * wq + H_D * DV_D + h * DV_D, 2 * wq + H_D * DV_D + (h + 1) * DV_D)
            v_pad = _pad_rows(z[:, vl].astype(BF16), LANES)
            s = _dot_nt(jnp.where(in_head, q, 0.0).astype(BF16), k_pad) * dmat_ref[h]
            o = _dot(s.astype(BF16), v_pad) + _dot(jnp.where(in_head, qd, 0.0).astype(BF16), s_all.astype(BF16))
            upd = _dot_tn(jnp.where((lane_pad // DK_D) == h, kd, jnp.zeros_like(kd)), v_pad)
            s_ref[sq, h] = s_all[rows] * dend_ref[rows] + upd[rows]
            o_ref[sq, :, h * DV_D:(h + 1) * DV_D] = (
                _rms(o, go_ref[...]) * (z[:, gl] * jax.nn.sigmoid(z[:, gl]))).astype(o_ref.dtype)


def _ret_sample(z_d, state, li, cos, sin, g_out):
    b, t_new, w = z_d.shape
    wq = H_D * DK_D
    dmat, dq, dk, dend = _ret_tables(t_new, t_new)
    dmat = jnp.pad(dmat, ((0, 0), (0, 0), (0, LANES - t_new)))
    const = lambda *shape: pl.BlockSpec(shape, lambda i: (0,) * len(shape))
    ns = SEQS_PER_STEP
    assert b % ns == 0
    return pl.pallas_call(
        _ret_sample_kernel,
        grid=(b // ns,),
        in_specs=[pl.BlockSpec((ns, t_new, w), lambda i: (i, 0, 0)),
                  pl.BlockSpec((1, ns, H_D, DK_D, DV_D), lambda i: (li, i, 0, 0, 0)),
                  const(t_new, wq), const(t_new, wq), const(H_D, t_new, LANES), const(t_new, wq), const(t_new, wq),
                  const(wq, 1), const(1, DV_D)],
        out_specs=[pl.BlockSpec((ns, t_new, BRANCH_W), lambda i: (i, 0, 0)),
                   pl.BlockSpec((ns, H_D, DK_D, DV_D), lambda i: (i, 0, 0, 0))],
        out_shape=[jax.ShapeDtypeStruct((b, t_new, BRANCH_W), F32),
                   jax.ShapeDtypeStruct((b, H_D, DK_D, DV_D), F32)],
        compiler_params=_params("parallel"),
    )(z_d, state, cos, sin, dmat, dq, dk, dend.reshape(wq, 1), g_out)


def _t5_bucket(rel):
    n = jnp.maximum(jnp.asarray(rel, jnp.int32), 0)
    large = T5_MAX_EXACT + (jnp.log(jnp.maximum(n, 1).astype(F32) / T5_MAX_EXACT)
                            / math.log(T5_MAX_DIST / T5_MAX_EXACT)
                            * (N_BUCKETS - T5_MAX_EXACT)).astype(jnp.int32)
    return jnp.where(n < T5_MAX_EXACT, n, jnp.minimum(large, N_BUCKETS - 1))


def _bias_lookup(table, buckets):
    out = jnp.zeros((table.shape[1],) + buckets.shape, F32)
    expand = (slice(None),) + (None,) * buckets.ndim
    for b in range(N_BUCKETS):
        out = jnp.where(buckets[None] == b, table[b][expand], out)
    return out


def _layer_weights(li, w_in, mla_w_qb, mla_w_uk, mla_w_uv, w_branch, w_out, w_gate, w_up, w_down):
    offs = np.cumsum([0, Q_LORA, KV_LORA, D_ROPE, H_B * 2 * DK_B, 2 * DK_B, DV_B, H_C * DK_C, H_C * DK_C, H_C * DV_C,
                      H_C * DV_C, H_D * DK_D, H_D * DK_D, H_D * DV_D, H_D * DV_D, N_BRANCH * D_MODEL])
    w = w_in[li]
    col = lambda a, b: w[:, offs[a]:offs[b]]
    w_a = jnp.concatenate([col(0, 2)] + [col(2, 3)] * H_A, axis=1)
    groups = [w_a, col(3, 6), col(6, 10), col(10, 14), col(14, 15)]
    wqb = mla_w_qb[li]
    return {
        "w_in": [g.astype(BF16) for g in groups],
        "w_nope": jnp.transpose(wqb[:, :, :D_NOPE], (1, 0, 2)).astype(BF16),
        "w_rope": wqb[:, :, D_NOPE:].reshape(Q_LORA, H_A * D_ROPE).astype(BF16),
        "w_ukT": jnp.transpose(mla_w_uk[li], (1, 2, 0)).astype(BF16),
        "w_uv": jnp.transpose(mla_w_uv[li], (1, 0, 2)).astype(BF16),
        "w_branch": w_branch[li].astype(BF16),
        "w_out": w_out[li].astype(BF16),
        "w_gate": w_gate[li].astype(BF16),
        "w_up": w_up[li].astype(BF16),
        "w_down": w_down[li].astype(BF16),
    }


def kernel(x_prompt, x_sample, cache_mla_latent, cache_mla_rope, cache_diff_k, cache_diff_v, state_hgrn, state_ret, page_table, t5_bias, norm_attn, w_in, mla_q_norm, mla_w_qb, mla_q_head_norm, mla_kv_norm, mla_krope_norm, mla_w_uk, mla_w_uv, diff_q_norm, diff_k_norm, diff_lambda, diff_out_norm, hgrn_lb_logits, hgrn_out_norm, ret_out_norm, w_branch, w_out, norm_ffn, w_gate, w_up, w_down):
    n_batch, t_p, _ = x_prompt.shape
    n_seq, t_s, _ = x_sample.shape
    assert n_batch == 1 and t_p % T_ATTN == 0
    past = page_table.shape[1] * PAGE_SIZE
    n_s = n_seq * t_s
    assert (t_p + n_s) % TM_DENSE == 0 and page_table.shape[1] % PAGES_PER_STEP == 0

    x = jnp.concatenate([x_prompt.reshape(t_p, D_MODEL), x_sample.reshape(n_s, D_MODEL)], axis=0)
    pos = jnp.concatenate([jnp.arange(t_p, dtype=jnp.int32), jnp.tile(past + jnp.arange(t_s, dtype=jnp.int32), n_seq)])
    cos, sin = _rope_tables(pos, H_A * D_ROPE)
    cos_s, sin_s = cos[t_p:t_p + t_s], sin[t_p:t_p + t_s]

    ta = np.arange(T_ATTN)
    rel = ta[:, None] - ta[None, :]
    bias_table = t5_bias.astype(F32) * LOG2E
    patch_buckets = jnp.stack([_t5_bucket(rel + T_ATTN), _t5_bucket(rel)])
    bias_patches = jnp.transpose(_bias_lookup(bias_table, patch_buckets), (1, 0, 2, 3))
    bias_far = _bias_lookup(bias_table, _t5_bucket(np.full((1,), 2 * T_ATTN))).reshape(H_B)
    q_pos = past + np.arange(t_s)
    strip_buckets = _t5_bucket(q_pos[:, None] - np.arange(past + PAGE_SIZE)[None, :])
    bias_strip = _bias_lookup(bias_table, strip_buckets).reshape(H_B * t_s, past + PAGE_SIZE)
    bias_past, bias_new = bias_strip[:, :past], bias_strip[:, past:]

    lb_cum = jnp.cumsum(jax.nn.softmax(hgrn_lb_logits.astype(F32), axis=0), axis=0)
    lb_all = lb_cum - lb_cum[:1]

    cache_rope_t = jnp.swapaxes(cache_mla_rope, 2, 3)
    cache_k = cache_diff_k.reshape(cache_diff_k.shape[:3] + (2 * DK_B,))
    cache_v = cache_diff_v.reshape(cache_diff_v.shape[:3] + (DV_B,))
    pad_new = lambda a: jnp.pad(a.reshape(n_seq, t_s, a.shape[-1]), ((0, 0), (0, PAGE_SIZE - t_s), (0, 0)))

    outs_p, outs_s = [], []
    for li in range(DEPTH):
        wts = _layer_weights(li, w_in, mla_w_qb, mla_w_uk, mla_w_uv, w_branch, w_out, w_gate, w_up, w_down)
        z_a, z_b, z_c, z_d, z_g = [
            _norm_matmul(x, norm_attn[li], w, tn)
            for w, tn in zip(wts["w_in"], (wts["w_in"][0].shape[1], wts["w_in"][1].shape[1], TN_DENSE, TN_DENSE, TN_DENSE))]

        pa = {"g_q": mla_q_norm[li].reshape(1, -1), "w_nope": wts["w_nope"], "w_rope": wts["w_rope"],
              "g_hn": mla_q_head_norm[li, :D_NOPE].reshape(1, -1),
              "g_hr": jnp.tile(mla_q_head_norm[li, D_NOPE:], H_A).reshape(1, -1), "w_ukT": wts["w_ukT"],
              "g_kv": mla_kv_norm[li].reshape(1, -1), "g_kr": jnp.tile(mla_krope_norm[li], H_A).reshape(1, -1)}
        qcat, kcat, c_kv, k_rope = _mla_prep(z_a, pa, cos, sin)
        o_a_p = _mla_prompt(qcat, kcat, wts["w_uv"], t_p)
        qs = qcat[:, t_p:].reshape(H_A, n_seq, t_s, -1)
        q_lat_s = jnp.transpose(qs[..., :KV_LORA], (1, 0, 2, 3)).reshape(n_seq, H_A * t_s, KV_LORA)
        q_rope_s = jnp.stack([qs[h, :, :, KV_LORA + h * D_ROPE:KV_LORA + (h + 1) * D_ROPE] for h in range(H_A)], axis=1)
        q_rope_s = q_rope_s.reshape(n_seq, H_A * t_s, D_ROPE)
        o_a_s = _mla_sample(page_table, q_lat_s, q_rope_s, pad_new(c_kv[t_p:]), pad_new(k_rope[t_p:]), wts["w_uv"],
                            cache_mla_latent, cache_rope_t, li)

        lam_init = 0.8 - 0.6 * math.exp(-0.3 * li)
        lv = diff_lambda[li].astype(F32)
        lam = (jnp.exp(jnp.sum(lv[0] * lv[1])) - jnp.exp(jnp.sum(lv[2] * lv[3])) + lam_init).reshape(1)
        qb, kb, vb = _diff_prep(z_b, jnp.tile(diff_q_norm[li], 2 * H_B).reshape(1, -1),
                                jnp.tile(diff_k_norm[li], 2).reshape(1, -1))
        g_diff = diff_out_norm[li].reshape(1, -1)
        o_b_p = _diff_prompt(qb, kb.astype(BF16), vb.astype(BF16), bias_patches, bias_far, lam, g_diff,
                             1.0 - lam_init, t_p)
        qbs = qb[t_p:].reshape(n_seq, t_s, H_B, 2, DK_B)
        zero = jnp.zeros_like(qbs[..., 0, :])
        q_pad = jnp.stack([jnp.concatenate([qbs[..., 0, :], zero], axis=-1),
                           jnp.concatenate([zero, qbs[..., 1, :]], axis=-1)], axis=1)
        q_pad = jnp.transpose(q_pad, (0, 1, 3, 2, 4)).reshape(n_seq, 2 * H_B * t_s, 2 * DK_B)
        o_b_s = _diff_sample(page_table, lam, q_pad, pad_new(kb[t_p:]), pad_new(vb[t_p:]), bias_past, bias_new, g_diff,
                             1.0 - lam_init, cache_k, cache_v, li)

        lb = lb_all[li].reshape(1, -1)
        a_lb, c_lb, oml = jnp.log(lb), jnp.log1p(-lb), 1.0 - lb
        g_hgrn = hgrn_out_norm[li].reshape(1, -1)
        o_c_p, s_hgrn_p = _hgrn_prompt(z_c, a_lb, c_lb, oml, g_hgrn, t_p)
        o_c_s, s_hgrn_s = _hgrn_sample(z_c[t_p:].reshape(n_seq, t_s, -1), state_hgrn, li, a_lb, c_lb, oml, g_hgrn)

        g_ret = ret_out_norm[li].reshape(1, -1)
        o_d_p, s_ret_p = _ret_prompt(z_d, cos, sin, g_ret, t_p)
        o_d_s, s_ret_s = _ret_sample(z_d[t_p:].reshape(n_seq, t_s, -1), state_ret, li, cos_s, sin_s, g_ret)

        branches = [jnp.concatenate([op, os_.reshape(n_s, BRANCH_W).astype(BF16)], axis=0)
                    for op, os_ in ((o_a_p, o_a_s), (o_b_p, o_b_s), (o_c_p, o_c_s), (o_d_p, o_d_s))]
        mixed = _merge(branches, wts["w_branch"], z_g)
        x = _matmul_res(mixed, wts["w_out"], x, TM_DENSE)
        u = _ffn_up(x, norm_ffn[li], wts["w_gate"], wts["w_up"])
        x = _matmul_res(u, wts["w_down"], x, TM_DENSE)

        outs_p.append((c_kv[:t_p].reshape(1, t_p, KV_LORA), k_rope[:t_p].reshape(1, t_p, D_ROPE),
                       kb[:t_p].reshape(1, t_p, 1, 2 * DK_B), vb[:t_p].reshape(1, t_p, 1, DV_B),
                       s_hgrn_p[None], s_ret_p[None]))
        outs_s.append((c_kv[t_p:].reshape(n_seq, t_s, KV_LORA), k_rope[t_p:].reshape(n_seq, t_s, D_ROPE),
                       kb[t_p:].reshape(n_seq, t_s, 1, 2 * DK_B), vb[t_p:].reshape(n_seq, t_s, 1, DV_B),
                       s_hgrn_s, s_ret_s))

    stack = lambda outs: [jnp.stack(a) for a in zip(*outs)]
    y_p = x[:t_p].reshape(1, t_p, D_MODEL)
    y_s = x[t_p:].reshape(n_seq, t_s, D_MODEL)
    return (y_p, y_s, *stack(outs_p), *stack(outs_s))
```

```python
import functools
import math

import numpy as np
import jax
import jax.numpy as jnp
from jax import lax
from jax.experimental import pallas as pl
from jax.experimental.pallas import tpu as pltpu

F32 = jnp.float32
BF16 = jnp.bfloat16

D_MODEL = 2048
DEPTH = 2
PAGE_SIZE = 128
N_BRANCH = 4
BRANCH_W = D_MODEL // 4
H_A, D_NOPE, D_ROPE, DV_A = 4, 128, 64, 128
Q_LORA, KV_LORA = 384, 256
MLA_SCALE = (D_NOPE + D_ROPE) ** -0.5
H_B, DK_B, DV_B = 4, 64, 128
DIFF_SCALE = DK_B ** -0.5
H_C, DK_C, DV_C = 4, 128, 128
H_D, DK_D, DV_D = 4, 64, 128
D_FF = 5632
N_BUCKETS, T5_MAX_EXACT, T5_MAX_DIST = 32, 16, 128
ROPE_THETA = 10000.0
EPS = 1e-6
LOG2E = math.log2(math.e)
MLA_QSCALE = MLA_SCALE * LOG2E
DIFF_QSCALE = DIFF_SCALE * LOG2E

LANES = 128
VMEM_LIMIT_BYTES = 56 * 2 ** 20

TM_DENSE = 1024
TN_DENSE = 512
TM_PREP = 256
T_ATTN = 512
T_REC = 128
SUB = 8
PAGES_PER_STEP = 32
PAGE_SLOTS = 3
SEQS_PER_STEP = 4
T_RET = 512
NEG_INF = float("-inf")


def _params(*sem):
    return pltpu.CompilerParams(dimension_semantics=sem, vmem_limit_bytes=VMEM_LIMIT_BYTES)


def _rms(x, gain):
    return x * lax.rsqrt(jnp.mean(x * x, axis=-1, keepdims=True) + EPS) * gain


def _dot(a, b):
    return jnp.dot(a, b, preferred_element_type=F32)


def _dot_nt(a, b):
    return lax.dot_general(a, b, (((1,), (1,)), ((), ())), preferred_element_type=F32)


def _dot_tn(a, b):
    return lax.dot_general(a, b, (((0,), (0,)), ((), ())), preferred_element_type=F32)


def _norm_matmul_kernel(x_ref, g_ref, w_ref, o_ref, h_ref):
    @pl.when(pl.program_id(1) == 0)
    def _():
        h_ref[...] = _rms(x_ref[...], g_ref[...]).astype(BF16)

    o_ref[...] = _dot(h_ref[...], w_ref[...]).astype(o_ref.dtype)


def _norm_matmul(x, gain, w, tn, out_dtype=F32):
    m, k = x.shape
    n = w.shape[1]
    tm = min(TM_DENSE, m)
    assert m % tm == 0 and n % tn == 0
    return pl.pallas_call(
        _norm_matmul_kernel,
        grid=(m // tm, n // tn),
        in_specs=[pl.BlockSpec((tm, k), lambda i, j: (i, 0)),
                  pl.BlockSpec((1, k), lambda i, j: (0, 0)),
                  pl.BlockSpec((k, tn), lambda i, j: (0, j))],
        out_specs=pl.BlockSpec((tm, tn), lambda i, j: (i, j)),
        out_shape=jax.ShapeDtypeStruct((m, n), out_dtype),
        scratch_shapes=[pltpu.VMEM((tm, k), BF16)],
        compiler_params=_params("parallel", "arbitrary"),
    )(x, gain.reshape(1, k), w)


def _ffn_up_kernel(x_ref, g_ref, wg_ref, wu_ref, o_ref, h_ref):
    @pl.when(pl.program_id(1) == 0)
    def _():
        h_ref[...] = _rms(x_ref[...], g_ref[...]).astype(BF16)

    h = h_ref[...]
    a = _dot(h, wg_ref[...])
    o_ref[...] = (a * jax.nn.sigmoid(a) * _dot(h, wu_ref[...])).astype(o_ref.dtype)


def _ffn_up(x, gain, wg, wu):
    m, k = x.shape
    n = wg.shape[1]
    tm, tn = min(TM_DENSE, m), TN_DENSE
    assert m % tm == 0 and n % tn == 0
    return pl.pallas_call(
        _ffn_up_kernel,
        grid=(m // tm, n // tn),
        in_specs=[pl.BlockSpec((tm, k), lambda i, j: (i, 0)),
                  pl.BlockSpec((1, k), lambda i, j: (0, 0)),
                  pl.BlockSpec((k, tn), lambda i, j: (0, j)),
                  pl.BlockSpec((k, tn), lambda i, j: (0, j))],
        out_specs=pl.BlockSpec((tm, tn), lambda i, j: (i, j)),
        out_shape=jax.ShapeDtypeStruct((m, n), BF16),
        scratch_shapes=[pltpu.VMEM((tm, k), BF16)],
        compiler_params=_params("parallel", "arbitrary"),
    )(x, gain.reshape(1, k), wg, wu)


def _matmul_res_kernel(a_ref, w_ref, r_ref, o_ref):
    o_ref[...] = r_ref[...] + _dot(a_ref[...], w_ref[...])


def _matmul_res(a, w, res, tm):
    m, k = a.shape
    n = w.shape[1]
    tn = TN_DENSE
    assert m % tm == 0 and n % tn == 0
    return pl.pallas_call(
        _matmul_res_kernel,
        grid=(m // tm, n // tn),
        in_specs=[pl.BlockSpec((tm, k), lambda i, j: (i, 0)),
                  pl.BlockSpec((k, tn), lambda i, j: (0, j)),
                  pl.BlockSpec((tm, tn), lambda i, j: (i, j))],
        out_specs=pl.BlockSpec((tm, tn), lambda i, j: (i, j)),
        out_shape=jax.ShapeDtypeStruct((m, n), F32),
        compiler_params=_params("parallel", "arbitrary"),
    )(a, w, res)


def _merge_kernel(oa_ref, ob_ref, oc_ref, od_ref, wb_ref, ga_ref, gb_ref, gc_ref, gd_ref, o_ref):
    acc = None
    for n, (o, g) in enumerate(((oa_ref, ga_ref), (ob_ref, gb_ref), (oc_ref, gc_ref), (od_ref, gd_ref))):
        t = jax.nn.sigmoid(g[...].astype(F32)) * _dot(o[...], wb_ref[n])
        acc = t if acc is None else acc + t
    o_ref[...] = acc.astype(o_ref.dtype)


def _merge(branches, w_branch, gate_logits):
    m = branches[0].shape[0]
    tm, tn = min(TM_DENSE, m), TN_DENSE
    nj = D_MODEL // tn
    bspec = pl.BlockSpec((tm, BRANCH_W), lambda i, j: (i, 0))
    gspecs = [pl.BlockSpec((tm, tn), functools.partial(lambda i, j, n: (i, n * nj + j), n=n))
              for n in range(N_BRANCH)]
    return pl.pallas_call(
        _merge_kernel,
        grid=(m // tm, nj),
        in_specs=[bspec] * N_BRANCH + [pl.BlockSpec((N_BRANCH, BRANCH_W, tn), lambda i, j: (0, 0, j))] + gspecs,
        out_specs=pl.BlockSpec((tm, tn), lambda i, j: (i, j)),
        out_shape=jax.ShapeDtypeStruct((m, D_MODEL), BF16),
        compiler_params=_params("parallel", "arbitrary"),
    )(*branches, w_branch, gate_logits, gate_logits, gate_logits, gate_logits)


def _rope64(x, cos, sin_signed):
    w = x.shape[-1]
    lane = lax.broadcasted_iota(jnp.int32, x.shape, x.ndim - 1)
    first_half = (lane % D_ROPE) < (D_ROPE // 2)
    partner = jnp.where(first_half, pltpu.roll(x, w - D_ROPE // 2, x.ndim - 1), pltpu.roll(x, D_ROPE // 2, x.ndim - 1))
    return x * cos + partner * sin_signed


def _rope_tables(pos, width):
    half = D_ROPE // 2
    inv = ROPE_THETA ** (-jnp.arange(half, dtype=F32) / half)
    ang = pos.astype(F32)[:, None] * inv[None, :]
    cos, sin = jnp.cos(ang), jnp.sin(ang)
    reps = width // D_ROPE
    return (jnp.tile(jnp.concatenate([cos, cos], axis=-1), (1, reps)),
            jnp.tile(jnp.concatenate([-sin, sin], axis=-1), (1, reps)))


def _mla_prep_kernel(z_ref, gq_ref, wn_ref, wr_ref, ghn_ref, ghr_ref, wuk_ref, gkv_ref, gkr_ref, cos_ref, sin_ref,
                     qcat_ref, kcat_ref, ckv_ref, krope_ref):
    z = z_ref[...]
    qn = _rms(z[:, :Q_LORA], gq_ref[...]).astype(BF16)
    qr = _dot(qn, wr_ref[...])
    lane = lax.broadcasted_iota(jnp.int32, qr.shape, 1)
    qr2 = qr * qr
    scale_lanes = jnp.zeros_like(qr)
    q_lat = []
    for h in range(H_A):
        in_head = (lane // D_ROPE) == h
        qh = _dot(qn, wn_ref[h])
        ss = jnp.sum(qh * qh, axis=-1, keepdims=True) + jnp.sum(jnp.where(in_head, qr2, 0.0), axis=-1, keepdims=True)
        r = lax.rsqrt(ss * (1.0 / (D_NOPE + D_ROPE)) + EPS)
        scale_lanes = jnp.where(in_head, r, scale_lanes)
        q_lat.append(_dot((qh * r * ghn_ref[...]).astype(BF16), wuk_ref[h]))
    cos, sin = cos_ref[...], sin_ref[...]
    qrr = _rope64(qr * scale_lanes * ghr_ref[...], cos, sin)
    for h in range(H_A):
        in_head = (lane // D_ROPE) == h
        qcat_ref[h] = jnp.concatenate([q_lat[h] * MLA_QSCALE, jnp.where(in_head, qrr, 0.0) * MLA_QSCALE],
                                      axis=-1).astype(BF16)
    ckv = _rms(z[:, Q_LORA:Q_LORA + KV_LORA], gkv_ref[...])
    krr = _rope64(_rms(z[:, Q_LORA + KV_LORA:], gkr_ref[...]), cos, sin)
    ckv_ref[...] = ckv
    krope_ref[...] = krr[:, :D_ROPE]
    kcat_ref[...] = jnp.concatenate([ckv, krr], axis=-1).astype(BF16)


def _mla_prep(z_a, p, cos, sin):
    m = z_a.shape[0]
    tm = TM_PREP
    wa = z_a.shape[1]
    wr = H_A * D_ROPE
    const = lambda *shape: pl.BlockSpec(shape, lambda i: (0,) * len(shape))
    return pl.pallas_call(
        _mla_prep_kernel,
        grid=(m // tm,),
        in_specs=[pl.BlockSpec((tm, wa), lambda i: (i, 0)),
                  const(1, Q_LORA), const(H_A, Q_LORA, D_NOPE), const(Q_LORA, wr), const(1, D_NOPE), const(1, wr),
                  const(H_A, D_NOPE, KV_LORA), const(1, KV_LORA), const(1, wr),
                  pl.BlockSpec((tm, wr), lambda i: (i, 0)), pl.BlockSpec((tm, wr), lambda i: (i, 0))],
        out_specs=[pl.BlockSpec((H_A, tm, KV_LORA + wr), lambda i: (0, i, 0)),
                   pl.BlockSpec((tm, KV_LORA + wr), lambda i: (i, 0)),
                   pl.BlockSpec((tm, KV_LORA), lambda i: (i, 0)),
                   pl.BlockSpec((tm, D_ROPE), lambda i: (i, 0))],
        out_shape=[jax.ShapeDtypeStruct((H_A, m, KV_LORA + wr), BF16),
                   jax.ShapeDtypeStruct((m, KV_LORA + wr), BF16),
                   jax.ShapeDtypeStruct((m, KV_LORA), F32),
                   jax.ShapeDtypeStruct((m, D_ROPE), F32)],
        compiler_params=_params("parallel"),
    )(z_a, p["g_q"], p["w_nope"], p["w_rope"], p["g_hn"], p["g_hr"], p["w_ukT"], p["g_kv"], p["g_kr"], cos, sin)


def _group_rms64(x, gain_tiled):
    lane = lax.broadcasted_iota(jnp.int32, x.shape, 1)
    x2 = x * x
    scale = jnp.zeros_like(x)
    for g in range(x.shape[1] // DK_B):
        in_g = (lane // DK_B) == g
        ss = jnp.sum(jnp.where(in_g, x2, 0.0), axis=-1, keepdims=True)
        scale = jnp.where(in_g, lax.rsqrt(ss * (1.0 / DK_B) + EPS), scale)
    return x * scale * gain_tiled


def _diff_prep_kernel(z_ref, gq_ref, gk_ref, q_ref, k_ref, v_ref):
    z = z_ref[...]
    nq = H_B * 2 * DK_B
    q_ref[...] = (_group_rms64(z[:, :nq], gq_ref[...]) * DIFF_QSCALE).astype(BF16)
    k_ref[...] = _group_rms64(z[:, nq:nq + 2 * DK_B], gk_ref[...])
    v_ref[...] = z[:, nq + 2 * DK_B:]


def _diff_prep(z_b, gq_tiled, gk_tiled):
    m, wb = z_b.shape
    tm = TM_PREP
    nq = H_B * 2 * DK_B
    return pl.pallas_call(
        _diff_prep_kernel,
        grid=(m // tm,),
        in_specs=[pl.BlockSpec((tm, wb), lambda i: (i, 0)),
                  pl.BlockSpec((1, nq), lambda i: (0, 0)), pl.BlockSpec((1, 2 * DK_B), lambda i: (0, 0))],
        out_specs=[pl.BlockSpec((tm, nq), lambda i: (i, 0)),
                   pl.BlockSpec((tm, 2 * DK_B), lambda i: (i, 0)),
                   pl.BlockSpec((tm, DV_B), lambda i: (i, 0))],
        out_shape=[jax.ShapeDtypeStruct((m, nq), BF16),
                   jax.ShapeDtypeStruct((m, 2 * DK_B), F32),
                   jax.ShapeDtypeStruct((m, DV_B), F32)],
        compiler_params=_params("parallel"),
    )(z_b, gq_tiled, gk_tiled)


def _widen(x, width):
    return x if width == LANES else jnp.concatenate([x] * (width // LANES), axis=1)


def _softmax_step(s, m_prev, l_prev, shift=None):
    m_cur = jnp.max(s, axis=-1, keepdims=True)
    if shift is not None:
        m_cur = m_cur + shift
    m_next = jnp.maximum(m_prev, m_cur)
    alpha = jnp.exp2(m_prev - m_next)
    p = jnp.exp2(s - _widen(m_next if shift is None else m_next - shift, s.shape[1]))
    l_next = alpha * l_prev + jnp.sum(p, axis=-1, keepdims=True)
    return p, m_next, l_next, alpha


def _init_softmax_state(m_ref, l_ref, acc_ref):
    m_ref[...] = jnp.full(m_ref.shape, NEG_INF, F32)
    l_ref[...] = jnp.zeros(l_ref.shape, F32)
    acc_ref[...] = jnp.zeros(acc_ref.shape, F32)


def _causal_tiles(n):
    pairs = [(i, j) for i in range(n) for j in range(i + 1)]
    return jnp.asarray([p[0] for p in pairs], jnp.int32), jnp.asarray([p[1] for p in pairs], jnp.int32)


def _mla_prompt_kernel(qi_ref, kj_ref, q_ref, k_ref, wuv_ref, o_ref, m_ref, l_ref, acc_ref):
    step = pl.program_id(0)
    i, j = qi_ref[step], kj_ref[step]
    t = k_ref.shape[0]

    @pl.when(j == 0)
    def _():
        _init_softmax_state(m_ref, l_ref, acc_ref)

    def tile(diagonal):
        k = k_ref[...]
        v = k[:, :KV_LORA]
        if diagonal:
            visible = lax.broadcasted_iota(jnp.int32, (t, t), 1) <= lax.broadcasted_iota(jnp.int32, (t, t), 0)
        for h in range(H_A):
            s = _dot_nt(q_ref[h], k)
            if diagonal:
                s = jnp.where(visible, s, NEG_INF)
            p, m_next, l_next, alpha = _softmax_step(s, m_ref[h], l_ref[h])
            m_ref[h] = m_next
            l_ref[h] = l_next
            acc_ref[h] = acc_ref[h] * _widen(alpha, KV_LORA) + _dot(p.astype(BF16), v)

    @pl.when(j < i)
    def _():
        tile(False)

    @pl.when(j == i)
    def _():
        tile(True)
        for h in range(H_A):
            o_lat = acc_ref[h] / _widen(l_ref[h], KV_LORA)
            o_ref[:, h * DV_A:(h + 1) * DV_A] = _dot(o_lat.astype(BF16), wuv_ref[h]).astype(o_ref.dtype)


def _mla_prompt(qcat, kcat, w_uv, t_len):
    t = T_ATTN
    qi, kj = _causal_tiles(t_len // t)
    dk = kcat.shape[1]
    return pl.pallas_call(
        _mla_prompt_kernel,
        grid_spec=pltpu.PrefetchScalarGridSpec(
            num_scalar_prefetch=2,
            grid=(qi.shape[0],),
            in_specs=[pl.BlockSpec((H_A, t, dk), lambda s, qi, kj: (0, qi[s], 0)),
                      pl.BlockSpec((t, dk), lambda s, qi, kj: (kj[s], 0)),
                      pl.BlockSpec((H_A, KV_LORA, DV_A), lambda s, qi, kj: (0, 0, 0))],
            out_specs=pl.BlockSpec((t, BRANCH_W), lambda s, qi, kj: (qi[s], 0)),
            scratch_shapes=[pltpu.VMEM((H_A, t, LANES), F32), pltpu.VMEM((H_A, t, LANES), F32),
                            pltpu.VMEM((H_A, t, KV_LORA), F32)]),
        out_shape=jax.ShapeDtypeStruct((t_len, BRANCH_W), BF16),
        compiler_params=_params("arbitrary"),
    )(qi, kj, qcat, kcat, w_uv)


def _diff_finish(acc0, l0, acc1, l1, lam, gain, out_scale):
    o = acc0 / l0 - lam * (acc1 / l1)
    return _rms(o, gain) * out_scale


def _diff_prompt_kernel(qi_ref, kj_ref, lam_ref, far_ref, q_ref, k_ref, v_ref, bias_ref, g_ref, o_ref,
                        m_ref, l_ref, acc_ref, *, out_scale):
    step = pl.program_id(0)
    i, j = qi_ref[step], kj_ref[step]
    t = k_ref.shape[0]

    @pl.when(j == 0)
    def _():
        _init_softmax_state(m_ref, l_ref, acc_ref)

    def tile(kind):
        k = k_ref[...]
        v = v_ref[...]
        lane = lax.broadcasted_iota(jnp.int32, (t, 2 * DK_B), 1)
        if kind == "diag":
            visible = lax.broadcasted_iota(jnp.int32, (t, t), 1) <= lax.broadcasted_iota(jnp.int32, (t, t), 0)
        for h in range(H_B):
            qh = q_ref[:, h * 2 * DK_B:(h + 1) * 2 * DK_B]
            for mp in range(2):
                qm = jnp.where((lane // DK_B) == mp, qh, jnp.zeros_like(qh))
                s = _dot_nt(qm, k)
                shift = None
                if kind == "far":
                    shift = far_ref[h]
                else:
                    s = s + bias_ref[0 if kind == "near" else 1, h]
                if kind == "diag":
                    s = jnp.where(visible, s, NEG_INF)
                idx = mp * H_B + h
                p, m_next, l_next, alpha = _softmax_step(s, m_ref[idx], l_ref[idx], shift)
                m_ref[idx] = m_next
                l_ref[idx] = l_next
                acc_ref[idx] = acc_ref[idx] * alpha + _dot(p.astype(BF16), v)

    @pl.when(j < i - 1)
    def _():
        tile("far")

    @pl.when(j == i - 1)
    def _():
        tile("near")

    @pl.when(j == i)
    def _():
        tile("diag")
        lam = lam_ref[0]
        for h in range(H_B):
            o = _diff_finish(acc_ref[h], l_ref[h], acc_ref[H_B + h], l_ref[H_B + h], lam, g_ref[...], out_scale)
            o_ref[:, h * DV_B:(h + 1) * DV_B] = o.astype(o_ref.dtype)


def _diff_prompt(qb, kb, vb, bias_patches, bias_far, lam, g_out, out_scale, t_len):
    t = T_ATTN
    assert t >= T5_MAX_DIST
    qi, kj = _causal_tiles(t_len // t)
    return pl.pallas_call(
        functools.partial(_diff_prompt_kernel, out_scale=out_scale),
        grid_spec=pltpu.PrefetchScalarGridSpec(
            num_scalar_prefetch=4,
            grid=(qi.shape[0],),
            in_specs=[pl.BlockSpec((t, H_B * 2 * DK_B), lambda s, qi, kj, lam, far: (qi[s], 0)),
                      pl.BlockSpec((t, 2 * DK_B), lambda s, qi, kj, lam, far: (kj[s], 0)),
                      pl.BlockSpec((t, DV_B), lambda s, qi, kj, lam, far: (kj[s], 0)),
                      pl.BlockSpec((2, H_B, t, t), lambda s, qi, kj, lam, far: (0, 0, 0, 0)),
                      pl.BlockSpec((1, DV_B), lambda s, qi, kj, lam, far: (0, 0))],
            out_specs=pl.BlockSpec((t, BRANCH_W), lambda s, qi, kj, lam, far: (qi[s], 0)),
            scratch_shapes=[pltpu.VMEM((2 * H_B, t, LANES), F32), pltpu.VMEM((2 * H_B, t, LANES), F32),
                            pltpu.VMEM((2 * H_B, t, DV_B), F32)]),
        out_shape=jax.ShapeDtypeStruct((t_len, BRANCH_W), BF16),
        compiler_params=_params("arbitrary"),
    )(qi, kj, lam, bias_far, qb, kb, vb, bias_patches, g_out)


def _new_token_mask(rows, t_new):
    row = lax.broadcasted_iota(jnp.int32, (rows, PAGE_SIZE), 0)
    col = lax.broadcasted_iota(jnp.int32, (rows, PAGE_SIZE), 1)
    return col <= (row % t_new)


def _softmax_pv(s_ref, v_ref, n_total):
    rows = s_ref.shape[1]
    unroll = PAGES_PER_STEP

    def page_max(n, m):
        return jnp.maximum(m, s_ref[n])

    m = lax.fori_loop(0, n_total, page_max, jnp.full((rows, PAGE_SIZE), NEG_INF, F32), unroll=unroll)
    m = jnp.max(m, axis=-1, keepdims=True)

    def page_pv(n, carry):
        l, acc = carry
        p = jnp.exp2(s_ref[n] - m)
        return l + p, acc + _dot(p.astype(BF16), v_ref[n])

    l, acc = lax.fori_loop(0, n_total, page_pv,
                           (jnp.zeros((rows, PAGE_SIZE), F32), jnp.zeros((rows, v_ref.shape[2]), F32)), unroll=unroll)
    return acc / jnp.sum(l, axis=-1, keepdims=True)


def _page_gather(pt_ref, hbm_a, hbm_b, buf_a, buf_b, sem, li, n_pages, n_seq, n_steps):
    ahead = PAGE_SLOTS - 1
    total = n_seq * n_steps
    assert total > ahead
    step = pl.program_id(0) * n_steps + pl.program_id(1)

    def copies(s):
        if isinstance(s, int):
            bb, jj, sl = s // n_steps, s % n_steps, s % PAGE_SLOTS
        else:
            bb, jj, sl = lax.div(s, n_steps), lax.rem(s, n_steps), lax.rem(s, PAGE_SLOTS)
        out = []
        for n in range(n_pages):
            page = pt_ref[bb, jj * n_pages + n]
            out.append((pltpu.make_async_copy(hbm_a.at[li, page], buf_a.at[sl, n], sem.at[0, sl]), n % 2))
            out.append((pltpu.make_async_copy(hbm_b.at[li, page], buf_b.at[sl, n], sem.at[1, sl]), n % 2))
        return out

    @pl.when(step == 0)
    def _():
        for s in range(ahead):
            for c, prio in copies(s):
                c.start(priority=prio)

    @pl.when(step + ahead < total)
    def _():
        for c, prio in copies(step + ahead):
            c.start(priority=prio)

    for c, _ in copies(step):
        c.wait()
    return lax.rem(step, PAGE_SLOTS)


def _mla_sample_kernel(pt_ref, ql_ref, qr_ref, nl_ref, nr_ref, wuv_ref, lat_hbm, rope_hbm, o_ref,
                       s_ref, v_ref, lat_buf, rope_buf, sem, *, n_pages, t_new, li, n_seq, n_steps):
    j = pl.program_id(1)
    ql, qr = ql_ref[0], qr_ref[0]
    n_past = n_pages * n_steps
    slot = _page_gather(pt_ref, lat_hbm, rope_hbm, lat_buf, rope_buf, sem, li, n_pages, n_seq, n_steps)
    for n in range(n_pages):
        lat = lat_buf[slot, n].astype(BF16)
        v_ref[j * n_pages + n] = lat
        s_ref[j * n_pages + n] = _dot_nt(ql, lat) + _dot(qr, rope_buf[slot, n].astype(BF16))

    @pl.when(j == pl.num_programs(1) - 1)
    def _():
        lat = nl_ref[0].astype(BF16)
        v_ref[n_past] = lat
        s_new = _dot_nt(ql, lat) + _dot_nt(qr, nr_ref[0].astype(BF16))
        s_ref[n_past] = jnp.where(_new_token_mask(ql.shape[0], t_new), s_new, NEG_INF)
        o_lat = _softmax_pv(s_ref, v_ref, n_past + 1)
        for h in range(H_A):
            o_ref[0, :, h * DV_A:(h + 1) * DV_A] = _dot(o_lat[h * t_new:(h + 1) * t_new].astype(BF16),
                                                      wuv_ref[h]).astype(o_ref.dtype)


def _mla_sample(page_table, q_lat, q_rope, new_lat, new_rope, w_uv, cache_lat, cache_rope_t, li):
    b, rows, _ = q_lat.shape
    t_new = rows // H_A
    n_pages = PAGES_PER_STEP
    total = page_table.shape[1]
    steps = total // n_pages
    per_seq = lambda r, w: pl.BlockSpec((1, r, w), lambda bi, j, pt: (bi, 0, 0))
    hbm = pl.BlockSpec(memory_space=pl.ANY)
    return pl.pallas_call(
        functools.partial(_mla_sample_kernel, n_pages=n_pages, t_new=t_new, li=li, n_seq=b, n_steps=steps),
        grid_spec=pltpu.PrefetchScalarGridSpec(
            num_scalar_prefetch=1,
            grid=(b, steps),
            in_specs=[per_seq(rows, KV_LORA), per_seq(rows, D_ROPE), per_seq(PAGE_SIZE, KV_LORA),
                      per_seq(PAGE_SIZE, D_ROPE),
                      pl.BlockSpec((H_A, KV_LORA, DV_A), lambda bi, j, pt: (0, 0, 0)), hbm, hbm],
            out_specs=pl.BlockSpec((1, t_new, BRANCH_W), lambda bi, j, pt: (bi, 0, 0)),
            scratch_shapes=[pltpu.VMEM((total + 1, rows, PAGE_SIZE), F32),
                            pltpu.VMEM((total + 1, PAGE_SIZE, KV_LORA), BF16),
                            pltpu.VMEM((PAGE_SLOTS, n_pages, PAGE_SIZE, KV_LORA), F32),
                            pltpu.VMEM((PAGE_SLOTS, n_pages, D_ROPE, PAGE_SIZE), F32),
                            pltpu.SemaphoreType.DMA((2, PAGE_SLOTS))]),
        out_shape=jax.ShapeDtypeStruct((b, t_new, BRANCH_W), F32),
        compiler_params=_params("arbitrary", "arbitrary"),
    )(page_table, q_lat, q_rope, new_lat, new_rope, w_uv, cache_lat, cache_rope_t)


def _diff_sample_kernel(pt_ref, lam_ref, q_ref, nk_ref, nv_ref, bias_ref, nbias_ref, g_ref, k_hbm, v_hbm, o_ref,
                        s_ref, vs_ref, k_buf, v_buf, sem, *, n_pages, t_new, out_scale, li, n_seq, n_steps):
    j = pl.program_id(1)
    q = q_ref[0]
    n_past = n_pages * n_steps
    slot = _page_gather(pt_ref, k_hbm, v_hbm, k_buf, v_buf, sem, li, n_pages, n_seq, n_steps)
    for n in range(n_pages):
        bias = bias_ref[:, n * PAGE_SIZE:(n + 1) * PAGE_SIZE]
        vs_ref[j * n_pages + n] = v_buf[slot, n].astype(BF16)
        s_ref[j * n_pages + n] = _dot_nt(q, k_buf[slot, n].astype(BF16)) + jnp.concatenate([bias, bias], axis=0)

    @pl.when(j == pl.num_programs(1) - 1)
    def _():
        nbias = nbias_ref[...]
        vs_ref[n_past] = nv_ref[0].astype(BF16)
        s_new = _dot_nt(q, nk_ref[0].astype(BF16)) + jnp.concatenate([nbias, nbias], axis=0)
        s_ref[n_past] = jnp.where(_new_token_mask(q.shape[0], t_new), s_new, NEG_INF)
        a = _softmax_pv(s_ref, vs_ref, n_past + 1)
        half = H_B * t_new
        o = _rms(a[:half] - lam_ref[0] * a[half:], g_ref[...]) * out_scale
        for h in range(H_B):
            o_ref[0, :, h * DV_B:(h + 1) * DV_B] = o[h * t_new:(h + 1) * t_new].astype(o_ref.dtype)


def _diff_sample(page_table, lam, q_pad, new_k, new_v, bias, new_bias, g_out, out_scale, cache_k, cache_v, li):
    b, rows, _ = q_pad.shape
    t_new = rows // (2 * H_B)
    n_pages = PAGES_PER_STEP
    total = page_table.shape[1]
    steps = total // n_pages
    per_seq = lambda r, w: pl.BlockSpec((1, r, w), lambda bi, j, pt, lm: (bi, 0, 0))
    hbm = pl.BlockSpec(memory_space=pl.ANY)
    return pl.pallas_call(
        functools.partial(_diff_sample_kernel, n_pages=n_pages, t_new=t_new, out_scale=out_scale, li=li, n_seq=b,
                          n_steps=steps),
        grid_spec=pltpu.PrefetchScalarGridSpec(
            num_scalar_prefetch=2,
            grid=(b, steps),
            in_specs=[per_seq(rows, 2 * DK_B), per_seq(PAGE_SIZE, 2 * DK_B), per_seq(PAGE_SIZE, DV_B),
                      pl.BlockSpec((rows // 2, n_pages * PAGE_SIZE), lambda bi, j, pt, lm: (0, j)),
                      pl.BlockSpec((rows // 2, PAGE_SIZE), lambda bi, j, pt, lm: (0, 0)),
                      pl.BlockSpec((1, DV_B), lambda bi, j, pt, lm: (0, 0)), hbm, hbm],
            out_specs=pl.BlockSpec((1, t_new, BRANCH_W), lambda bi, j, pt, lm: (bi, 0, 0)),
            scratch_shapes=[pltpu.VMEM((total + 1, rows, PAGE_SIZE), F32),
                            pltpu.VMEM((total + 1, PAGE_SIZE, DV_B), BF16),
                            pltpu.VMEM((PAGE_SLOTS, n_pages, PAGE_SIZE, 2 * DK_B), F32),
                            pltpu.VMEM((PAGE_SLOTS, n_pages, PAGE_SIZE, DV_B), F32),
                            pltpu.SemaphoreType.DMA((2, PAGE_SLOTS))]),
        out_shape=jax.ShapeDtypeStruct((b, t_new, BRANCH_W), F32),
        compiler_params=_params("arbitrary", "arbitrary"),
    )(page_table, lam, q_pad, new_k, new_v, bias, new_bias, g_out, cache_k, cache_v)


def _cumsum_rows(x):
    n = x.shape[0]
    row = lax.broadcasted_iota(jnp.int32, x.shape, 0)
    sh = 1
    while sh < n:
        x = x + jnp.where(row >= sh, pltpu.roll(x, sh, 0), 0.0)
        sh *= 2
    return x


def _group_row(x, group, j):
    n, w = x.shape
    if group == n:
        return jnp.broadcast_to(x[j:j + 1, :], (n, w))
    x3 = x.reshape(n // group, group, w)
    return jnp.broadcast_to(x3[:, j:j + 1, :], x3.shape).reshape(n, w)


def _hgrn_gates(cq, cf, a_lb, c_lb, oml):
    q = cq * jax.nn.sigmoid(cq)
    log_sig = jnp.minimum(cf, 0.0) - jnp.log1p(jnp.exp(-jnp.abs(cf)))
    x = c_lb + log_sig
    g = jnp.maximum(a_lb, x) + jnp.log1p(jnp.exp(-jnp.abs(a_lb - x)))
    k = oml * jax.nn.sigmoid(-cf)
    return q, k, g


def _hgrn_tile(q, k, v, g, st, sub):
    n = q.shape[0]
    b = _cumsum_rows(g)
    row = lax.broadcasted_iota(jnp.int32, (n, 1), 0)
    o = _dot_nt((q * jnp.exp(b)).astype(BF16), st.astype(BF16))
    for j in range(sub):
        e = jnp.where((row % sub) >= j, b - _group_row(b, sub, j), NEG_INF)
        r = jnp.sum(q * _group_row(k, sub, j) * jnp.exp(e), axis=-1, keepdims=True)
        o = o + r * _group_row(v, sub, j)
    half = sub
    while half < n:
        upper = (row % (2 * half)) >= half
        bb = _group_row(b, 2 * half, half - 1)
        qe = q * jnp.exp(jnp.where(upper, b - bb, NEG_INF))
        ke = k * jnp.exp(jnp.where(upper, NEG_INF, bb - b))
        s = _dot_nt(qe.astype(BF16), ke.astype(BF16))
        trow = lax.broadcasted_iota(jnp.int32, (n, n), 0) // (2 * half)
        tcol = lax.broadcasted_iota(jnp.int32, (n, n), 1) // (2 * half)
        o = o + _dot(jnp.where(trow == tcol, s, 0.0).astype(BF16), v.astype(BF16))
        half *= 2
    b_last = b[n - 1:n, :]
    kd = k * jnp.exp(b_last - b)
    st_next = st * jnp.exp(b_last) + _dot_tn(v.astype(BF16), kd.astype(BF16))
    return o, st_next


def _hgrn_prompt_kernel(cq_ref, cf_ref, ci_ref, cg_ref, a_ref, c_ref, oml_ref, go_ref, o_ref, s_ref, st_ref):
    t = pl.program_id(0)

    @pl.when(t == 0)
    def _():
        st_ref[...] = jnp.zeros(st_ref.shape, F32)

    for h in range(H_C):
        hl = slice(h * DK_C, (h + 1) * DK_C)
        q, k, g = _hgrn_gates(cq_ref[:, hl], cf_ref[:, hl], a_ref[:, hl], c_ref[:, hl], oml_ref[:, hl])
        o, st_next = _hgrn_tile(q, k, ci_ref[:, hl], g, st_ref[h], SUB)
        st_ref[h] = st_next
        o_ref[:, hl] = (_rms(o, go_ref[...]) * jax.nn.sigmoid(cg_ref[:, hl])).astype(o_ref.dtype)

        @pl.when(t == pl.num_programs(0) - 1)
        def _():
            s_ref[h] = st_next.T


def _hgrn_prompt(z_c, a_lb, c_lb, oml, g_out, t_len):
    tt = T_REC
    w = H_C * DK_C
    sect = lambda s: pl.BlockSpec((tt, w), functools.partial(lambda t, s: (t, s), s=s))
    vec = pl.BlockSpec((1, w), lambda t: (0, 0))
    return pl.pallas_call(
        _hgrn_prompt_kernel,
        grid=(t_len // tt,),
        in_specs=[sect(0), sect(1), sect(2), sect(3), vec, vec, vec, pl.BlockSpec((1, DV_C), lambda t: (0, 0))],
        out_specs=[pl.BlockSpec((tt, BRANCH_W), lambda t: (t, 0)),
                   pl.BlockSpec((H_C, DK_C, DV_C), lambda t: (0, 0, 0))],
        out_shape=[jax.ShapeDtypeStruct((t_len, BRANCH_W), BF16),
                   jax.ShapeDtypeStruct((H_C, DK_C, DV_C), F32)],
        scratch_shapes=[pltpu.VMEM((H_C, DV_C, DK_C), F32)],
        compiler_params=_params("arbitrary"),
    )(z_c, z_c, z_c, z_c, a_lb, c_lb, oml, g_out)


def _pad_rows(x, n):
    return jnp.concatenate([x, jnp.zeros((n - x.shape[0], x.shape[1]), x.dtype)], axis=0)


def _hgrn_sample_kernel(z_ref, s0_ref, a_ref, c_ref, oml_ref, go_ref, o_ref, s_ref):
    n = z_ref.shape[1]
    w = H_C * DK_C
    row = lax.broadcasted_iota(jnp.int32, (n, 1), 0)
    for sq in range(z_ref.shape[0]):
        z = z_ref[sq]
        for h in range(H_C):
            lanes = lambda s: slice(s * w + h * DK_C, s * w + (h + 1) * DK_C)
            hl = slice(h * DK_C, (h + 1) * DK_C)
            q, k, g = _hgrn_gates(z[:, lanes(0)], z[:, lanes(1)], a_ref[:, hl], c_ref[:, hl], oml_ref[:, hl])
            v = z[:, lanes(2)]
            s0 = s0_ref[0, sq, h]
            b = _cumsum_rows(g)
            o = _dot((q * jnp.exp(b)).astype(BF16), s0.astype(BF16))
            for j in range(n):
                e = jnp.where(row >= j, b - b[j:j + 1, :], NEG_INF)
                r = jnp.sum(q * k[j:j + 1, :] * jnp.exp(e), axis=-1, keepdims=True)
                o = o + r * v[j:j + 1, :]
            b_last = b[n - 1:n, :]
            kd = _pad_rows((k * jnp.exp(b_last - b)).astype(BF16), LANES)
            st_next = s0.T * jnp.exp(b_last) + _dot_tn(_pad_rows(v.astype(BF16), LANES), kd)
            s_ref[sq, h] = st_next.T
            o_ref[sq, :, hl] = (_rms(o, go_ref[...]) * jax.nn.sigmoid(z[:, lanes(3)])).astype(o_ref.dtype)


def _hgrn_sample(z_c, state, li, a_lb, c_lb, oml, g_out):
    b, t_new, w = z_c.shape
    ns = SEQS_PER_STEP
    assert b % ns == 0
    vec = pl.BlockSpec((1, H_C * DK_C), lambda i: (0, 0))
    return pl.pallas_call(
        _hgrn_sample_kernel,
        grid=(b // ns,),
        in_specs=[pl.BlockSpec((ns, t_new, w), lambda i: (i, 0, 0)),
                  pl.BlockSpec((1, ns, H_C, DK_C, DV_C), lambda i: (li, i, 0, 0, 0)),
                  vec, vec, vec, pl.BlockSpec((1, DV_C), lambda i: (0, 0))],
        out_specs=[pl.BlockSpec((ns, t_new, BRANCH_W), lambda i: (i, 0, 0)),
                   pl.BlockSpec((ns, H_C, DK_C, DV_C), lambda i: (i, 0, 0, 0))],
        out_shape=[jax.ShapeDtypeStruct((b, t_new, BRANCH_W), F32),
                   jax.ShapeDtypeStruct((b, H_C, DK_C, DV_C), F32)],
        compiler_params=_params("parallel"),
    )(z_c, state, a_lb, c_lb, oml, g_out)


def _ret_log_decay():
    return np.log1p(-np.exp2(-5.0 - np.arange(H_D, dtype=np.float64)))


def _ret_tables(n, block):
    lg = _ret_log_decay()
    t = np.arange(n)
    loc = t % block
    same = (t[:, None] // block) == (t[None, :] // block)
    diff = loc[:, None] - loc[None, :]
    dmat = np.where(same & (diff >= 0), np.exp(lg[:, None, None] * np.maximum(diff, 0)[None]), 0.0)
    per_lane = lambda f: np.repeat(f, DK_D, axis=1)
    dq = per_lane(np.exp(lg[None, :] * (loc[:, None] + 1)))
    dk = per_lane(np.exp(lg[None, :] * (block - 1 - loc[:, None])))
    dend = per_lane(np.exp(lg[None, :] * block))
    return tuple(jnp.asarray(a, F32) for a in (dmat, dq, dk, dend))


def _ret_prompt_kernel(rq_ref, rk_ref, rv_ref, rg_ref, cos_ref, sin_ref, dmat_ref, dq_ref, dk_ref, dend_ref, go_ref,
                       o_ref, s_ref, st_ref):
    t = pl.program_id(0)

    @pl.when(t == 0)
    def _():
        st_ref[...] = jnp.zeros(st_ref.shape, F32)

    cos, sin = cos_ref[...], sin_ref[...]
    q = _rope64(rq_ref[...], cos, sin)
    k = _rope64(rk_ref[...], cos, sin) * (DK_D ** -0.5)
    lane = lax.broadcasted_iota(jnp.int32, q.shape, 1)
    k_bf = k.astype(BF16)
    qd = q * dq_ref[...]
    kd = k * dk_ref[...]
    for h in range(H_D):
        in_head = (lane // DK_D) == h
        vl = slice(h * DV_D, (h + 1) * DV_D)
        v = rv_ref[:, vl].astype(BF16)
        st = st_ref[h]
        s = _dot_nt(jnp.where(in_head, q, 0.0).astype(BF16), k_bf) * dmat_ref[h]
        o = _dot(s.astype(BF16), v) + _dot_nt(jnp.where(in_head, qd, 0.0).astype(BF16), st.astype(BF16))
        st_next = st * dend_ref[...] + _dot_tn(v, jnp.where(in_head, kd, 0.0).astype(BF16))
        st_ref[h] = st_next
        o_ref[:, vl] = (_rms(o, go_ref[...]) * (rg_ref[:, vl] * jax.nn.sigmoid(rg_ref[:, vl]))).astype(o_ref.dtype)

        @pl.when(t == pl.num_programs(0) - 1)
        def _():
            s_ref[h] = st_next.T[h * DK_D:(h + 1) * DK_D, :]


def _ret_prompt(z_d, cos, sin, g_out, t_len):
    tt = T_RET
    wq = H_D * DK_D
    wv = H_D * DV_D
    dmat, dq, dk, dend = _ret_tables(tt, tt)
    const = lambda *shape: pl.BlockSpec(shape, lambda t: (0,) * len(shape))
    return pl.pallas_call(
        _ret_prompt_kernel,
        grid=(t_len // tt,),
        in_specs=[pl.BlockSpec((tt, wq), lambda t: (t, 0)), pl.BlockSpec((tt, wq), lambda t: (t, 1)),
                  pl.BlockSpec((tt, wv), lambda t: (t, 1)), pl.BlockSpec((tt, wv), lambda t: (t, 2)),
                  pl.BlockSpec((tt, wq), lambda t: (t, 0)), pl.BlockSpec((tt, wq), lambda t: (t, 0)),
                  const(H_D, tt, tt), const(tt, wq), const(tt, wq), const(1, wq), const(1, DV_D)],
        out_specs=[pl.BlockSpec((tt, BRANCH_W), lambda t: (t, 0)),
                   pl.BlockSpec((H_D, DK_D, DV_D), lambda t: (0, 0, 0))],
        out_shape=[jax.ShapeDtypeStruct((t_len, BRANCH_W), BF16),
                   jax.ShapeDtypeStruct((H_D, DK_D, DV_D), F32)],
        scratch_shapes=[pltpu.VMEM((H_D, DV_D, wq), F32)],
        compiler_params=_params("arbitrary"),
    )(z_d, z_d, z_d, z_d, cos, sin, dmat, dq, dk, dend, g_out)


def _ret_sample_kernel(z_ref, s0_ref, cos_ref, sin_ref, dmat_ref, dq_ref, dk_ref, dend_ref, go_ref, o_ref, s_ref):
    wq = H_D * DK_D
    cos, sin = cos_ref[...], sin_ref[...]
    for sq in range(z_ref.shape[0]):
        z = z_ref[sq]
        q = _rope64(z[:, :wq], cos, sin)
        k = _rope64(z[:, wq:2 * wq], cos, sin) * (DK_D ** -0.5)
        lane = lax.broadcasted_iota(jnp.int32, q.shape, 1)
        k_pad = _pad_rows(k.astype(BF16), LANES)
        qd = q * dq_ref[...]
        kd = _pad_rows((k * dk_ref[...]).astype(BF16), LANES)
        lane_pad = lax.broadcasted_iota(jnp.int32, kd.shape, 1)
        s_all = s0_ref[0, sq].reshape(wq, DV_D)
        for h in range(H_D):
            in_head = (lane // DK_D) == h
            rows = slice(h * DK_D, (h + 1) * DK_D)
            vl = slice(2 * wq + h * DV_D, 2 * wq + (h + 1) * DV_D)
            gl = slice(2 * wq + H_D * DV_D + h * DV_D, 2 * wq + H_D * DV_D + (h + 1) * DV_D)
            v_pad = _pad_rows(z[:, vl].astype(BF16), LANES)
            s = _dot_nt(jnp.where(in_head, q, 0.0).astype(BF16), k_pad) * dmat_ref[h]
            o = _dot(s.astype(BF16), v_pad) + _dot(jnp.where(in_head, qd, 0.0).astype(BF16), s_all.astype(BF16))
            upd = _dot_tn(jnp.where((lane_pad // DK_D) == h, kd, jnp.zeros_like(kd)), v_pad)
            s_ref[sq, h] = s_all[rows] * dend_ref[rows] + upd[rows]
            o_ref[sq, :, h * DV_D:(h + 1) * DV_D] = (
                _rms(o, go_ref[...]) * (z[:, gl] * jax.nn.sigmoid(z[:, gl]))).astype(o_ref.dtype)


def _ret_sample(z_d, state, li, cos, sin, g_out):
    b, t_new, w = z_d.shape
    wq = H_D * DK_D
    dmat, dq, dk, dend = _ret_tables(t_new, t_new)
    dmat = jnp.pad(dmat, ((0, 0), (0, 0), (0, LANES - t_new)))
    const = lambda *shape: pl.BlockSpec(shape, lambda i: (0,) * len(shape))
    ns = SEQS_PER_STEP
    assert b % ns == 0
    return pl.pallas_call(
        _ret_sample_kernel,
        grid=(b // ns,),
        in_specs=[pl.BlockSpec((ns, t_new, w), lambda i: (i, 0, 0)),
                  pl.BlockSpec((1, ns, H_D, DK_D, DV_D), lambda i: (li, i, 0, 0, 0)),
                  const(t_new, wq), const(t_new, wq), const(H_D, t_new, LANES), const(t_new, wq), const(t_new, wq),
                  const(wq, 1), const(1, DV_D)],
        out_specs=[pl.BlockSpec((ns, t_new, BRANCH_W), lambda i: (i, 0, 0)),
                   pl.BlockSpec((ns, H_D, DK_D, DV_D), lambda i: (i, 0, 0, 0))],
        out_shape=[jax.ShapeDtypeStruct((b, t_new, BRANCH_W), F32),
                   jax.ShapeDtypeStruct((b, H_D, DK_D, DV_D), F32)],
        compiler_params=_params("parallel"),
    )(z_d, state, cos, sin, dmat, dq, dk, dend.reshape(wq, 1), g_out)


def _t5_bucket(rel):
    n = jnp.maximum(jnp.asarray(rel, jnp.int32), 0)
    large = T5_MAX_EXACT + (jnp.log(jnp.maximum(n, 1).astype(F32) / T5_MAX_EXACT)
                            / math.log(T5_MAX_DIST / T5_MAX_EXACT)
                            * (N_BUCKETS - T5_MAX_EXACT)).astype(jnp.int32)
    return jnp.where(n < T5_MAX_EXACT, n, jnp.minimum(large, N_BUCKETS - 1))


def _bias_lookup(table, buckets):
    out = jnp.zeros((table.shape[1],) + buckets.shape, F32)
    expand = (slice(None),) + (None,) * buckets.ndim
    for b in range(N_BUCKETS):
        out = jnp.where(buckets[None] == b, table[b][expand], out)
    return out


def _layer_weights(li, w_in, mla_w_qb, mla_w_uk, mla_w_uv, w_branch, w_out, w_gate, w_up, w_down):
    offs = np.cumsum([0, Q_LORA, KV_LORA, D_ROPE, H_B * 2 * DK_B, 2 * DK_B, DV_B, H_C * DK_C, H_C * DK_C, H_C * DV_C,
                      H_C * DV_C, H_D * DK_D, H_D * DK_D, H_D * DV_D, H_D * DV_D, N_BRANCH * D_MODEL])
    w = w_in[li]
    col = lambda a, b: w[:, offs[a]:offs[b]]
    w_a = jnp.concatenate([col(0, 2)] + [col(2, 3)] * H_A, axis=1)
    groups = [w_a, col(3, 6), col(6, 10), col(10, 14), col(14, 15)]
    wqb = mla_w_qb[li]
    return {
        "w_in": [g.astype(BF16) for g in groups],
        "w_nope": jnp.transpose(wqb[:, :, :D_NOPE], (1, 0, 2)).astype(BF16),
        "w_rope": wqb[:, :, D_NOPE:].reshape(Q_LORA, H_A * D_ROPE).astype(BF16),
        "w_ukT": jnp.transpose(mla_w_uk[li], (1, 2, 0)).astype(BF16),
        "w_uv": jnp.transpose(mla_w_uv[li], (1, 0, 2)).astype(BF16),
        "w_branch": w_branch[li].astype(BF16),
        "w_out": w_out[li].astype(BF16),
        "w_gate": w_gate[li].astype(BF16),
        "w_up": w_up[li].astype(BF16),
        "w_down": w_down[li].astype(BF16),
    }


def kernel(x_prompt, x_sample, cache_mla_latent, cache_mla_rope, cache_diff_k, cache_diff_v, state_hgrn, state_ret, page_table, t5_bias, norm_attn, w_in, mla_q_norm, mla_w_qb, mla_q_head_norm, mla_kv_norm, mla_krope_norm, mla_w_uk, mla_w_uv, diff_q_norm, diff_k_norm, diff_lambda, diff_out_norm, hgrn_lb_logits, hgrn_out_norm, ret_out_norm, w_branch, w_out, norm_ffn, w_gate, w_up, w_down):
    n_batch, t_p, _ = x_prompt.shape
    n_seq, t_s, _ = x_sample.shape
    assert n_batch == 1 and t_p % T_ATTN == 0
    past = page_table.shape[1] * PAGE_SIZE
    n_s = n_seq * t_s
    assert (t_p + n_s) % TM_DENSE == 0 and page_table.shape[1] % PAGES_PER_STEP == 0

    x = jnp.concatenate([x_prompt.reshape(t_p, D_MODEL), x_sample.reshape(n_s, D_MODEL)], axis=0)
    pos = jnp.concatenate([jnp.arange(t_p, dtype=jnp.int32), jnp.tile(past + jnp.arange(t_s, dtype=jnp.int32), n_seq)])
    cos, sin = _rope_tables(pos, H_A * D_ROPE)
    cos_s, sin_s = cos[t_p:t_p + t_s], sin[t_p:t_p + t_s]

    ta = np.arange(T_ATTN)
    rel = ta[:, None] - ta[None, :]
    bias_table = t5_bias.astype(F32) * LOG2E
    patch_buckets = jnp.stack([_t5_bucket(rel + T_ATTN), _t5_bucket(rel)])
    bias_patches = jnp.transpose(_bias_lookup(bias_table, patch_buckets), (1, 0, 2, 3))
    bias_far = _bias_lookup(bias_table, _t5_bucket(np.full((1,), 2 * T_ATTN))).reshape(H_B)
    q_pos = past + np.arange(t_s)
    strip_buckets = _t5_bucket(q_pos[:, None] - np.arange(past + PAGE_SIZE)[None, :])
    bias_strip = _bias_lookup(bias_table, strip_buckets).reshape(H_B * t_s, past + PAGE_SIZE)
    bias_past, bias_new = bias_strip[:, :past], bias_strip[:, past:]

    lb_cum = jnp.cumsum(jax.nn.softmax(hgrn_lb_logits.astype(F32), axis=0), axis=0)
    lb_all = lb_cum - lb_cum[:1]

    cache_rope_t = jnp.swapaxes(cache_mla_rope, 2, 3)
    cache_k = cache_diff_k.reshape(cache_diff_k.shape[:3] + (2 * DK_B,))
    cache_v = cache_diff_v.reshape(cache_diff_v.shape[:3] + (DV_B,))
    pad_new = lambda a: jnp.pad(a.reshape(n_seq, t_s, a.shape[-1]), ((0, 0), (0, PAGE_SIZE - t_s), (0, 0)))

    outs_p, outs_s = [], []
    for li in range(DEPTH):
        wts = _layer_weights(li, w_in, mla_w_qb, mla_w_uk, mla_w_uv, w_branch, w_out, w_gate, w_up, w_down)
        z_a, z_b, z_c, z_d, z_g = [
            _norm_matmul(x, norm_attn[li], w, tn, dt)
            for w, tn, dt in zip(wts["w_in"],
                                 (wts["w_in"][0].shape[1], wts["w_in"][1].shape[1], TN_DENSE, TN_DENSE, 2 * TN_DENSE),
                                 (F32, F32, F32, F32, BF16))]

        pa = {"g_q": mla_q_norm[li].reshape(1, -1), "w_nope": wts["w_nope"], "w_rope": wts["w_rope"],
              "g_hn": mla_q_head_norm[li, :D_NOPE].reshape(1, -1),
              "g_hr": jnp.tile(mla_q_head_norm[li, D_NOPE:], H_A).reshape(1, -1), "w_ukT": wts["w_ukT"],
              "g_kv": mla_kv_norm[li].reshape(1, -1), "g_kr": jnp.tile(mla_krope_norm[li], H_A).reshape(1, -1)}
        qcat, kcat, c_kv, k_rope = _mla_prep(z_a, pa, cos, sin)
        o_a_p = _mla_prompt(qcat, kcat, wts["w_uv"], t_p)
        qs = qcat[:, t_p:].reshape(H_A, n_seq, t_s, -1)
        q_lat_s = jnp.transpose(qs[..., :KV_LORA], (1, 0, 2, 3)).reshape(n_seq, H_A * t_s, KV_LORA)
        q_rope_s = jnp.stack([qs[h, :, :, KV_LORA + h * D_ROPE:KV_LORA + (h + 1) * D_ROPE] for h in range(H_A)], axis=1)
        q_rope_s = q_rope_s.reshape(n_seq, H_A * t_s, D_ROPE)
        o_a_s = _mla_sample(page_table, q_lat_s, q_rope_s, pad_new(c_kv[t_p:]), pad_new(k_rope[t_p:]), wts["w_uv"],
                            cache_mla_latent, cache_rope_t, li)

        lam_init = 0.8 - 0.6 * math.exp(-0.3 * li)
        lv = diff_lambda[li].astype(F32)
        lam = (jnp.exp(jnp.sum(lv[0] * lv[1])) - jnp.exp(jnp.sum(lv[2] * lv[3])) + lam_init).reshape(1)
        qb, kb, vb = _diff_prep(z_b, jnp.tile(diff_q_norm[li], 2 * H_B).reshape(1, -1),
                                jnp.tile(diff_k_norm[li], 2).reshape(1, -1))
        g_diff = diff_out_norm[li].reshape(1, -1)
        o_b_p = _diff_prompt(qb, kb.astype(BF16), vb.astype(BF16), bias_patches, bias_far, lam, g_diff,
                             1.0 - lam_init, t_p)
        qbs = qb[t_p:].reshape(n_seq, t_s, H_B, 2, DK_B)
        zero = jnp.zeros_like(qbs[..., 0, :])
        q_pad = jnp.stack([jnp.concatenate([qbs[..., 0, :], zero], axis=-1),
                           jnp.concatenate([zero, qbs[..., 1, :]], axis=-1)], axis=1)
        q_pad = jnp.transpose(q_pad, (0, 1, 3, 2, 4)).reshape(n_seq, 2 * H_B * t_s, 2 * DK_B)
        o_b_s = _diff_sample(page_table, lam, q_pad, pad_new(kb[t_p:]), pad_new(vb[t_p:]), bias_past, bias_new, g_diff,
                             1.0 - lam_init, cache_k, cache_v, li)

        lb = lb_all[li].reshape(1, -1)
        a_lb, c_lb, oml = jnp.log(lb), jnp.log1p(-lb), 1.0 - lb
        g_hgrn = hgrn_out_norm[li].reshape(1, -1)
        o_c_p, s_hgrn_p = _hgrn_prompt(z_c, a_lb, c_lb, oml, g_hgrn, t_p)
        o_c_s, s_hgrn_s = _hgrn_sample(z_c[t_p:].reshape(n_seq, t_s, -1), state_hgrn, li, a_lb, c_lb, oml, g_hgrn)

        g_ret = ret_out_norm[li].reshape(1, -1)
        o_d_p, s_ret_p = _ret_prompt(z_d, cos, sin, g_ret, t_p)
        o_d_s, s_ret_s = _ret_sample(z_d[t_p:].reshape(n_seq, t_s, -1), state_ret, li, cos_s, sin_s, g_ret)

        branches = [jnp.concatenate([op, os_.reshape(n_s, BRANCH_W).astype(BF16)], axis=0)
                    for op, os_ in ((o_a_p, o_a_s), (o_b_p, o_b_s), (o_c_p, o_c_s), (o_d_p, o_d_s))]
        mixed = _merge(branches, wts["w_branch"], z_g)
        x = _matmul_res(mixed, wts["w_out"], x, TM_DENSE)
        u = _ffn_up(x, norm_ffn[li], wts["w_gate"], wts["w_up"])
        x = _matmul_res(u, wts["w_down"], x, TM_DENSE)

        outs_p.append((c_kv[:t_p].reshape(1, t_p, KV_LORA), k_rope[:t_p].reshape(1, t_p, D_ROPE),
                       kb[:t_p].reshape(1, t_p, 1, 2 * DK_B), vb[:t_p].reshape(1, t_p, 1, DV_B),
                       s_hgrn_p[None], s_ret_p[None]))
        outs_s.append((c_kv[t_p:].reshape(n_seq, t_s, KV_LORA), k_rope[t_p:].reshape(n_seq, t_s, D_ROPE),
                       kb[t_p:].reshape(n_seq, t_s, 1, 2 * DK_B), vb[t_p:].reshape(n_seq, t_s, 1, DV_B),
                       s_hgrn_s, s_ret_s))

    stack = lambda outs: [jnp.stack(a) for a in zip(*outs)]
    y_p = x[:t_p].reshape(1, t_p, D_MODEL)
    y_s = x[t_p:].reshape(n_seq, t_s, D_MODEL)
    return (y_p, y_s, *stack(outs_p), *stack(outs_s))
```

```python
import functools
import math

import numpy as np
import jax
import jax.numpy as jnp
from jax import lax
from jax.experimental import pallas as pl
from jax.experimental.pallas import tpu as pltpu

F32 = jnp.float32
BF16 = jnp.bfloat16

D_MODEL = 2048
DEPTH = 2
PAGE_SIZE = 128
N_BRANCH = 4
BRANCH_W = D_MODEL // 4
H_A, D_NOPE, D_ROPE, DV_A = 4, 128, 64, 128
Q_LORA, KV_LORA = 384, 256
MLA_SCALE = (D_NOPE + D_ROPE) ** -0.5
H_B, DK_B, DV_B = 4, 64, 128
DIFF_SCALE = DK_B ** -0.5
H_C, DK_C, DV_C = 4, 128, 128
H_D, DK_D, DV_D = 4, 64, 128
D_FF = 5632
N_BUCKETS, T5_MAX_EXACT, T5_MAX_DIST = 32, 16, 128
ROPE_THETA = 10000.0
EPS = 1e-6
LOG2E = math.log2(math.e)
MLA_QSCALE = MLA_SCALE * LOG2E
DIFF_QSCALE = DIFF_SCALE * LOG2E

LANES = 128
VMEM_LIMIT_BYTES = 56 * 2 ** 20

TM_DENSE = 1024
TN_DENSE = 512
TM_PREP = 256
T_ATTN = 512
T_REC = 128
SUB = 8
PAGES_PER_STEP = 32
PAGE_SLOTS = 3
SEQS_PER_STEP = 4
T_RET = 512
NEG_INF = float("-inf")


def _params(*sem):
    return pltpu.CompilerParams(dimension_semantics=sem, vmem_limit_bytes=VMEM_LIMIT_BYTES)


def _rms(x, gain):
    return x * lax.rsqrt(jnp.mean(x * x, axis=-1, keepdims=True) + EPS) * gain


def _dot(a, b):
    return jnp.dot(a, b, preferred_element_type=F32)


def _dot_nt(a, b):
    return lax.dot_general(a, b, (((1,), (1,)), ((), ())), preferred_element_type=F32)


def _dot_tn(a, b):
    return lax.dot_general(a, b, (((0,), (0,)), ((), ())), preferred_element_type=F32)


def _norm_matmul_kernel(x_ref, g_ref, w_ref, o_ref, h_ref):
    @pl.when(pl.program_id(1) == 0)
    def _():
        h_ref[...] = _rms(x_ref[...], g_ref[...]).astype(BF16)

    o_ref[...] = _dot(h_ref[...], w_ref[...]).astype(o_ref.dtype)


def _norm_matmul(x, gain, w, tn, out_dtype=F32):
    m, k = x.shape
    n = w.shape[1]
    tm = min(TM_DENSE, m)
    assert m % tm == 0 and n % tn == 0
    return pl.pallas_call(
        _norm_matmul_kernel,
        grid=(m // tm, n // tn),
        in_specs=[pl.BlockSpec((tm, k), lambda i, j: (i, 0)),
                  pl.BlockSpec((1, k), lambda i, j: (0, 0)),
                  pl.BlockSpec((k, tn), lambda i, j: (0, j))],
        out_specs=pl.BlockSpec((tm, tn), lambda i, j: (i, j)),
        out_shape=jax.ShapeDtypeStruct((m, n), out_dtype),
        scratch_shapes=[pltpu.VMEM((tm, k), BF16)],
        compiler_params=_params("parallel", "arbitrary"),
    )(x, gain.reshape(1, k), w)


def _ffn_up_kernel(x_ref, g_ref, wg_ref, wu_ref, o_ref, h_ref):
    @pl.when(pl.program_id(1) == 0)
    def _():
        h_ref[...] = _rms(x_ref[...], g_ref[...]).astype(BF16)

    h = h_ref[...]
    a = _dot(h, wg_ref[...])
    o_ref[...] = (a * jax.nn.sigmoid(a) * _dot(h, wu_ref[...])).astype(o_ref.dtype)


def _ffn_up(x, gain, wg, wu):
    m, k = x.shape
    n = wg.shape[1]
    tm, tn = min(TM_DENSE, m), TN_DENSE
    assert m % tm == 0 and n % tn == 0
    return pl.pallas_call(
        _ffn_up_kernel,
        grid=(m // tm, n // tn),
        in_specs=[pl.BlockSpec((tm, k), lambda i, j: (i, 0)),
                  pl.BlockSpec((1, k), lambda i, j: (0, 0)),
                  pl.BlockSpec((k, tn), lambda i, j: (0, j)),
                  pl.BlockSpec((k, tn), lambda i, j: (0, j))],
        out_specs=pl.BlockSpec((tm, tn), lambda i, j: (i, j)),
        out_shape=jax.ShapeDtypeStruct((m, n), BF16),
        scratch_shapes=[pltpu.VMEM((tm, k), BF16)],
        compiler_params=_params("parallel", "arbitrary"),
    )(x, gain.reshape(1, k), wg, wu)


def _matmul_res_kernel(a_ref, w_ref, r_ref, o_ref):
    o_ref[...] = r_ref[...] + _dot(a_ref[...], w_ref[...])


def _matmul_res(a, w, res, tm, tn):
    m, k = a.shape
    n = w.shape[1]
    assert m % tm == 0 and n % tn == 0
    return pl.pallas_call(
        _matmul_res_kernel,
        grid=(m // tm, n // tn),
        in_specs=[pl.BlockSpec((tm, k), lambda i, j: (i, 0)),
                  pl.BlockSpec((k, tn), lambda i, j: (0, j)),
                  pl.BlockSpec((tm, tn), lambda i, j: (i, j))],
        out_specs=pl.BlockSpec((tm, tn), lambda i, j: (i, j)),
        out_shape=jax.ShapeDtypeStruct((m, n), F32),
        compiler_params=_params("parallel", "arbitrary"),
    )(a, w, res)


def _merge_kernel(*refs, prompt_tiles):
    p_refs, s_refs = refs[:N_BRANCH], refs[N_BRANCH:2 * N_BRANCH]
    wb_ref = refs[2 * N_BRANCH]
    g_refs = refs[2 * N_BRANCH + 1:3 * N_BRANCH + 1]
    o_ref = refs[3 * N_BRANCH + 1]
    i = pl.program_id(0)

    def merge(branch):
        acc = None
        for n in range(N_BRANCH):
            t = jax.nn.sigmoid(g_refs[n][...].astype(F32)) * _dot(branch(n), wb_ref[n])
            acc = t if acc is None else acc + t
        o_ref[...] = acc.astype(o_ref.dtype)

    @pl.when(i < prompt_tiles)
    def _():
        merge(lambda n: p_refs[n][...])

    @pl.when(i >= prompt_tiles)
    def _():
        merge(lambda n: s_refs[n][...].astype(BF16))


def _merge(prompt_branches, sample_branches, w_branch, gate_logits):
    t_p, n_s = prompt_branches[0].shape[0], sample_branches[0].shape[0]
    tm, tn = TM_DENSE, TN_DENSE
    assert t_p % tm == 0 and n_s % tm == 0
    pt, st = t_p // tm, n_s // tm
    nj = D_MODEL // tn
    pspec = pl.BlockSpec((tm, BRANCH_W), lambda i, j: (jnp.minimum(i, pt - 1), 0))
    sspec = pl.BlockSpec((tm, BRANCH_W), lambda i, j: (jnp.maximum(i - pt, 0), 0))
    gspecs = [pl.BlockSpec((tm, tn), functools.partial(lambda i, j, n: (i, n * nj + j), n=n))
              for n in range(N_BRANCH)]
    return pl.pallas_call(
        functools.partial(_merge_kernel, prompt_tiles=pt),
        grid=(pt + st, nj),
        in_specs=[pspec] * N_BRANCH + [sspec] * N_BRANCH
        + [pl.BlockSpec((N_BRANCH, BRANCH_W, tn), lambda i, j: (0, 0, j))] + gspecs,
        out_specs=pl.BlockSpec((tm, tn), lambda i, j: (i, j)),
        out_shape=jax.ShapeDtypeStruct((t_p + n_s, D_MODEL), BF16),
        compiler_params=_params("parallel", "arbitrary"),
    )(*prompt_branches, *sample_branches, w_branch, gate_logits, gate_logits, gate_logits, gate_logits)


def _rope64(x, cos, sin_signed):
    w = x.shape[-1]
    lane = lax.broadcasted_iota(jnp.int32, x.shape, x.ndim - 1)
    first_half = (lane % D_ROPE) < (D_ROPE // 2)
    partner = jnp.where(first_half, pltpu.roll(x, w - D_ROPE // 2, x.ndim - 1), pltpu.roll(x, D_ROPE // 2, x.ndim - 1))
    return x * cos + partner * sin_signed


def _rope_tables(pos, width):
    half = D_ROPE // 2
    inv = ROPE_THETA ** (-jnp.arange(half, dtype=F32) / half)
    ang = pos.astype(F32)[:, None] * inv[None, :]
    cos, sin = jnp.cos(ang), jnp.sin(ang)
    reps = width // D_ROPE
    return (jnp.tile(jnp.concatenate([cos, cos], axis=-1), (1, reps)),
            jnp.tile(jnp.concatenate([-sin, sin], axis=-1), (1, reps)))


def _mla_prep_kernel(z_ref, gq_ref, wn_ref, wr_ref, ghn_ref, ghr_ref, wuk_ref, gkv_ref, gkr_ref, cos_ref, sin_ref,
                     qcat_ref, kcat_ref, ckv_ref, krope_ref):
    z = z_ref[...]
    qn = _rms(z[:, :Q_LORA], gq_ref[...]).astype(BF16)
    qr = _dot(qn, wr_ref[...])
    lane = lax.broadcasted_iota(jnp.int32, qr.shape, 1)
    qr2 = qr * qr
    scale_lanes = jnp.zeros_like(qr)
    q_lat = []
    for h in range(H_A):
        in_head = (lane // D_ROPE) == h
        qh = _dot(qn, wn_ref[h])
        ss = jnp.sum(qh * qh, axis=-1, keepdims=True) + jnp.sum(jnp.where(in_head, qr2, 0.0), axis=-1, keepdims=True)
        r = lax.rsqrt(ss * (1.0 / (D_NOPE + D_ROPE)) + EPS)
        scale_lanes = jnp.where(in_head, r, scale_lanes)
        q_lat.append(_dot((qh * r * ghn_ref[...]).astype(BF16), wuk_ref[h]))
    cos, sin = cos_ref[...], sin_ref[...]
    qrr = _rope64(qr * scale_lanes * ghr_ref[...], cos, sin)
    for h in range(H_A):
        in_head = (lane // D_ROPE) == h
        qcat_ref[h] = jnp.concatenate([q_lat[h] * MLA_QSCALE, jnp.where(in_head, qrr, 0.0) * MLA_QSCALE],
                                      axis=-1).astype(BF16)
    ckv = _rms(z[:, Q_LORA:Q_LORA + KV_LORA], gkv_ref[...])
    krr = _rope64(_rms(z[:, Q_LORA + KV_LORA:], gkr_ref[...]), cos, sin)
    ckv_ref[...] = ckv
    krope_ref[...] = krr[:, :D_ROPE]
    kcat_ref[...] = jnp.concatenate([ckv, krr], axis=-1).astype(BF16)


def _mla_prep(z_a, p, cos, sin):
    m = z_a.shape[0]
    tm = TM_PREP
    wa = z_a.shape[1]
    wr = H_A * D_ROPE
    const = lambda *shape: pl.BlockSpec(shape, lambda i: (0,) * len(shape))
    return pl.pallas_call(
        _mla_prep_kernel,
        grid=(m // tm,),
        in_specs=[pl.BlockSpec((tm, wa), lambda i: (i, 0)),
                  const(1, Q_LORA), const(H_A, Q_LORA, D_NOPE), const(Q_LORA, wr), const(1, D_NOPE), const(1, wr),
                  const(H_A, D_NOPE, KV_LORA), const(1, KV_LORA), const(1, wr),
                  pl.BlockSpec((tm, wr), lambda i: (i, 0)), pl.BlockSpec((tm, wr), lambda i: (i, 0))],
        out_specs=[pl.BlockSpec((H_A, tm, KV_LORA + wr), lambda i: (0, i, 0)),
                   pl.BlockSpec((tm, KV_LORA + wr), lambda i: (i, 0)),
                   pl.BlockSpec((tm, KV_LORA), lambda i: (i, 0)),
                   pl.BlockSpec((tm, D_ROPE), lambda i: (i, 0))],
        out_shape=[jax.ShapeDtypeStruct((H_A, m, KV_LORA + wr), BF16),
                   jax.ShapeDtypeStruct((m, KV_LORA + wr), BF16),
                   jax.ShapeDtypeStruct((m, KV_LORA), F32),
                   jax.ShapeDtypeStruct((m, D_ROPE), F32)],
        compiler_params=_params("parallel"),
    )(z_a, p["g_q"], p["w_nope"], p["w_rope"], p["g_hn"], p["g_hr"], p["w_ukT"], p["g_kv"], p["g_kr"], cos, sin)


def _group_rms64(x, gain_tiled):
    lane = lax.broadcasted_iota(jnp.int32, x.shape, 1)
    x2 = x * x
    scale = jnp.zeros_like(x)
    for g in range(x.shape[1] // DK_B):
        in_g = (lane // DK_B) == g
        ss = jnp.sum(jnp.where(in_g, x2, 0.0), axis=-1, keepdims=True)
        scale = jnp.where(in_g, lax.rsqrt(ss * (1.0 / DK_B) + EPS), scale)
    return x * scale * gain_tiled


def _diff_prep_kernel(z_ref, gq_ref, gk_ref, q_ref, k_ref, v_ref):
    z = z_ref[...]
    nq = H_B * 2 * DK_B
    q_ref[...] = (_group_rms64(z[:, :nq], gq_ref[...]) * DIFF_QSCALE).astype(BF16)
    k_ref[...] = _group_rms64(z[:, nq:nq + 2 * DK_B], gk_ref[...])
    v_ref[...] = z[:, nq + 2 * DK_B:]


def _diff_prep(z_b, gq_tiled, gk_tiled):
    m, wb = z_b.shape
    tm = TM_PREP
    nq = H_B * 2 * DK_B
    return pl.pallas_call(
        _diff_prep_kernel,
        grid=(m // tm,),
        in_specs=[pl.BlockSpec((tm, wb), lambda i: (i, 0)),
                  pl.BlockSpec((1, nq), lambda i: (0, 0)), pl.BlockSpec((1, 2 * DK_B), lambda i: (0, 0))],
        out_specs=[pl.BlockSpec((tm, nq), lambda i: (i, 0)),
                   pl.BlockSpec((tm, 2 * DK_B), lambda i: (i, 0)),
                   pl.BlockSpec((tm, DV_B), lambda i: (i, 0))],
        out_shape=[jax.ShapeDtypeStruct((m, nq), BF16),
                   jax.ShapeDtypeStruct((m, 2 * DK_B), F32),
                   jax.ShapeDtypeStruct((m, DV_B), F32)],
        compiler_params=_params("parallel"),
    )(z_b, gq_tiled, gk_tiled)


def _widen(x, width):
    return x if width == LANES else jnp.concatenate([x] * (width // LANES), axis=1)


def _softmax_step(s, m_prev, l_prev, shift=None):
    m_cur = jnp.max(s, axis=-1, keepdims=True)
    if shift is not None:
        m_cur = m_cur + shift
    m_next = jnp.maximum(m_prev, m_cur)
    alpha = jnp.exp2(m_prev - m_next)
    p = jnp.exp2(s - _widen(m_next if shift is None else m_next - shift, s.shape[1]))
    l_next = alpha * l_prev + jnp.sum(p, axis=-1, keepdims=True)
    return p, m_next, l_next, alpha


def _init_softmax_state(m_ref, l_ref, acc_ref):
    m_ref[...] = jnp.full(m_ref.shape, NEG_INF, F32)
    l_ref[...] = jnp.zeros(l_ref.shape, F32)
    acc_ref[...] = jnp.zeros(acc_ref.shape, F32)


def _causal_tiles(n):
    pairs = [(i, j) for i in range(n) for j in range(i + 1)]
    return jnp.asarray([p[0] for p in pairs], jnp.int32), jnp.asarray([p[1] for p in pairs], jnp.int32)


def _mla_prompt_kernel(qi_ref, kj_ref, q_ref, k_ref, wuv_ref, o_ref, m_ref, l_ref, acc_ref):
    step = pl.program_id(0)
    i, j = qi_ref[step], kj_ref[step]
    t = k_ref.shape[0]

    @pl.when(j == 0)
    def _():
        _init_softmax_state(m_ref, l_ref, acc_ref)

    def tile(diagonal):
        k = k_ref[...]
        v = k[:, :KV_LORA]
        if diagonal:
            visible = lax.broadcasted_iota(jnp.int32, (t, t), 1) <= lax.broadcasted_iota(jnp.int32, (t, t), 0)
        for h in range(H_A):
            s = _dot_nt(q_ref[h], k)
            if diagonal:
                s = jnp.where(visible, s, NEG_INF)
            p, m_next, l_next, alpha = _softmax_step(s, m_ref[h], l_ref[h])
            m_ref[h] = m_next
            l_ref[h] = l_next
            acc_ref[h] = acc_ref[h] * _widen(alpha, KV_LORA) + _dot(p.astype(BF16), v)

    @pl.when(j < i)
    def _():
        tile(False)

    @pl.when(j == i)
    def _():
        tile(True)
        for h in range(H_A):
            o_lat = acc_ref[h] / _widen(l_ref[h], KV_LORA)
            o_ref[:, h * DV_A:(h + 1) * DV_A] = _dot(o_lat.astype(BF16), wuv_ref[h]).astype(o_ref.dtype)


def _mla_prompt(qcat, kcat, w_uv, t_len):
    t = T_ATTN
    qi, kj = _causal_tiles(t_len // t)
    dk = kcat.shape[1]
    return pl.pallas_call(
        _mla_prompt_kernel,
        grid_spec=pltpu.PrefetchScalarGridSpec(
            num_scalar_prefetch=2,
            grid=(qi.shape[0],),
            in_specs=[pl.BlockSpec((H_A, t, dk), lambda s, qi, kj: (0, qi[s], 0)),
                      pl.BlockSpec((t, dk), lambda s, qi, kj: (kj[s], 0)),
                      pl.BlockSpec((H_A, KV_LORA, DV_A), lambda s, qi, kj: (0, 0, 0))],
            out_specs=pl.BlockSpec((t, BRANCH_W), lambda s, qi, kj: (qi[s], 0)),
            scratch_shapes=[pltpu.VMEM((H_A, t, LANES), F32), pltpu.VMEM((H_A, t, LANES), F32),
                            pltpu.VMEM((H_A, t, KV_LORA), F32)]),
        out_shape=jax.ShapeDtypeStruct((t_len, BRANCH_W), BF16),
        compiler_params=_params("arbitrary"),
    )(qi, kj, qcat, kcat, w_uv)


def _diff_finish(acc0, l0, acc1, l1, lam, gain, out_scale):
    o = acc0 / l0 - lam * (acc1 / l1)
    return _rms(o, gain) * out_scale


def _diff_prompt_kernel(qi_ref, kj_ref, lam_ref, far_ref, q_ref, k_ref, v_ref, bias_ref, g_ref, o_ref,
                        m_ref, l_ref, acc_ref, *, out_scale):
    step = pl.program_id(0)
    i, j = qi_ref[step], kj_ref[step]
    t = k_ref.shape[0]

    @pl.when(j == 0)
    def _():
        _init_softmax_state(m_ref, l_ref, acc_ref)

    def tile(kind):
        k = k_ref[...]
        v = v_ref[...]
        lane = lax.broadcasted_iota(jnp.int32, (t, 2 * DK_B), 1)
        if kind == "diag":
            visible = lax.broadcasted_iota(jnp.int32, (t, t), 1) <= lax.broadcasted_iota(jnp.int32, (t, t), 0)
        for h in range(H_B):
            qh = q_ref[:, h * 2 * DK_B:(h + 1) * 2 * DK_B]
            for mp in range(2):
                qm = jnp.where((lane // DK_B) == mp, qh, jnp.zeros_like(qh))
                s = _dot_nt(qm, k)
                shift = None
                if kind == "far":
                    shift = far_ref[h]
                else:
                    s = s + bias_ref[0 if kind == "near" else 1, h]
                if kind == "diag":
                    s = jnp.where(visible, s, NEG_INF)
                idx = mp * H_B + h
                p, m_next, l_next, alpha = _softmax_step(s, m_ref[idx], l_ref[idx], shift)
                m_ref[idx] = m_next
                l_ref[idx] = l_next
                acc_ref[idx] = acc_ref[idx] * alpha + _dot(p.astype(BF16), v)

    @pl.when(j < i - 1)
    def _():
        tile("far")

    @pl.when(j == i - 1)
    def _():
        tile("near")

    @pl.when(j == i)
    def _():
        tile("diag")
        lam = lam_ref[0]
        for h in range(H_B):
            o = _diff_finish(acc_ref[h], l_ref[h], acc_ref[H_B + h], l_ref[H_B + h], lam, g_ref[...], out_scale)
            o_ref[:, h * DV_B:(h + 1) * DV_B] = o.astype(o_ref.dtype)


def _diff_prompt(qb, kb, vb, bias_patches, bias_far, lam, g_out, out_scale, t_len):
    t = T_ATTN
    assert t >= T5_MAX_DIST
    qi, kj = _causal_tiles(t_len // t)
    return pl.pallas_call(
        functools.partial(_diff_prompt_kernel, out_scale=out_scale),
        grid_spec=pltpu.PrefetchScalarGridSpec(
            num_scalar_prefetch=4,
            grid=(qi.shape[0],),
            in_specs=[pl.BlockSpec((t, H_B * 2 * DK_B), lambda s, qi, kj, lam, far: (qi[s], 0)),
                      pl.BlockSpec((t, 2 * DK_B), lambda s, qi, kj, lam, far: (kj[s], 0)),
                      pl.BlockSpec((t, DV_B), lambda s, qi, kj, lam, far: (kj[s], 0)),
                      pl.BlockSpec((2, H_B, t, t), lambda s, qi, kj, lam, far: (0, 0, 0, 0)),
                      pl.BlockSpec((1, DV_B), lambda s, qi, kj, lam, far: (0, 0))],
            out_specs=pl.BlockSpec((t, BRANCH_W), lambda s, qi, kj, lam, far: (qi[s], 0)),
            scratch_shapes=[pltpu.VMEM((2 * H_B, t, LANES), F32), pltpu.VMEM((2 * H_B, t, LANES), F32),
                            pltpu.VMEM((2 * H_B, t, DV_B), F32)]),
        out_shape=jax.ShapeDtypeStruct((t_len, BRANCH_W), BF16),
        compiler_params=_params("arbitrary"),
    )(qi, kj, lam, bias_far, qb, kb, vb, bias_patches, g_out)


def _new_token_mask(rows, t_new):
    row = lax.broadcasted_iota(jnp.int32, (rows, PAGE_SIZE), 0)
    col = lax.broadcasted_iota(jnp.int32, (rows, PAGE_SIZE), 1)
    return col <= (row % t_new)


def _softmax_pv(s_ref, v_ref, n_total):
    rows = s_ref.shape[1]
    unroll = PAGES_PER_STEP

    def page_max(n, m):
        return jnp.maximum(m, s_ref[n])

    m = lax.fori_loop(0, n_total, page_max, jnp.full((rows, PAGE_SIZE), NEG_INF, F32), unroll=unroll)
    m = jnp.max(m, axis=-1, keepdims=True)

    def page_pv(n, carry):
        l, acc = carry
        p = jnp.exp2(s_ref[n] - m)
        return l + p, acc + _dot(p.astype(BF16), v_ref[n])

    l, acc = lax.fori_loop(0, n_total, page_pv,
                           (jnp.zeros((rows, PAGE_SIZE), F32), jnp.zeros((rows, v_ref.shape[2]), F32)), unroll=unroll)
    return acc / jnp.sum(l, axis=-1, keepdims=True)


def _page_gather(pt_ref, hbm_a, hbm_b, buf_a, buf_b, sem, li, n_pages, n_seq, n_steps):
    ahead = PAGE_SLOTS - 1
    total = n_seq * n_steps
    assert total > ahead
    step = pl.program_id(0) * n_steps + pl.program_id(1)

    def copies(s):
        if isinstance(s, int):
            bb, jj, sl = s // n_steps, s % n_steps, s % PAGE_SLOTS
        else:
            bb, jj, sl = lax.div(s, n_steps), lax.rem(s, n_steps), lax.rem(s, PAGE_SLOTS)
        out = []
        for n in range(n_pages):
            page = pt_ref[bb, jj * n_pages + n]
            out.append((pltpu.make_async_copy(hbm_a.at[li, page], buf_a.at[sl, n], sem.at[0, sl]), n % 2))
            out.append((pltpu.make_async_copy(hbm_b.at[li, page], buf_b.at[sl, n], sem.at[1, sl]), n % 2))
        return out

    @pl.when(step == 0)
    def _():
        for s in range(ahead):
            for c, prio in copies(s):
                c.start(priority=prio)

    @pl.when(step + ahead < total)
    def _():
        for c, prio in copies(step + ahead):
            c.start(priority=prio)

    for c, _ in copies(step):
        c.wait()
    return lax.rem(step, PAGE_SLOTS)


def _mla_sample_kernel(pt_ref, ql_ref, qr_ref, nl_ref, nr_ref, wuv_ref, lat_hbm, rope_hbm, o_ref,
                       s_ref, v_ref, lat_buf, rope_buf, sem, *, n_pages, t_new, li, n_seq, n_steps):
    j = pl.program_id(1)
    ql, qr = ql_ref[0], qr_ref[0]
    n_past = n_pages * n_steps
    slot = _page_gather(pt_ref, lat_hbm, rope_hbm, lat_buf, rope_buf, sem, li, n_pages, n_seq, n_steps)
    for n in range(n_pages):
        lat = lat_buf[slot, n].astype(BF16)
        v_ref[j * n_pages + n] = lat
        s_ref[j * n_pages + n] = _dot_nt(ql, lat) + _dot(qr, rope_buf[slot, n].astype(BF16))

    @pl.when(j == pl.num_programs(1) - 1)
    def _():
        lat = nl_ref[0].astype(BF16)
        v_ref[n_past] = lat
        s_new = _dot_nt(ql, lat) + _dot_nt(qr, nr_ref[0].astype(BF16))
        s_ref[n_past] = jnp.where(_new_token_mask(ql.shape[0], t_new), s_new, NEG_INF)
        o_lat = _softmax_pv(s_ref, v_ref, n_past + 1)
        for h in range(H_A):
            o_ref[0, :, h * DV_A:(h + 1) * DV_A] = _dot(o_lat[h * t_new:(h + 1) * t_new].astype(BF16),
                                                      wuv_ref[h]).astype(o_ref.dtype)


def _mla_sample(page_table, q_lat, q_rope, new_lat, new_rope, w_uv, cache_lat, cache_rope_t, li):
    b, rows, _ = q_lat.shape
    t_new = rows // H_A
    n_pages = PAGES_PER_STEP
    total = page_table.shape[1]
    steps = total // n_pages
    per_seq = lambda r, w: pl.BlockSpec((1, r, w), lambda bi, j, pt: (bi, 0, 0))
    hbm = pl.BlockSpec(memory_space=pl.ANY)
    return pl.pallas_call(
        functools.partial(_mla_sample_kernel, n_pages=n_pages, t_new=t_new, li=li, n_seq=b, n_steps=steps),
        grid_spec=pltpu.PrefetchScalarGridSpec(
            num_scalar_prefetch=1,
            grid=(b, steps),
            in_specs=[per_seq(rows, KV_LORA), per_seq(rows, D_ROPE), per_seq(PAGE_SIZE, KV_LORA),
                      per_seq(PAGE_SIZE, D_ROPE),
                      pl.BlockSpec((H_A, KV_LORA, DV_A), lambda bi, j, pt: (0, 0, 0)), hbm, hbm],
            out_specs=pl.BlockSpec((1, t_new, BRANCH_W), lambda bi, j, pt: (bi, 0, 0)),
            scratch_shapes=[pltpu.VMEM((total + 1, rows, PAGE_SIZE), F32),
                            pltpu.VMEM((total + 1, PAGE_SIZE, KV_LORA), BF16),
                            pltpu.VMEM((PAGE_SLOTS, n_pages, PAGE_SIZE, KV_LORA), F32),
                            pltpu.VMEM((PAGE_SLOTS, n_pages, D_ROPE, PAGE_SIZE), F32),
                            pltpu.SemaphoreType.DMA((2, PAGE_SLOTS))]),
        out_shape=jax.ShapeDtypeStruct((b, t_new, BRANCH_W), F32),
        compiler_params=_params("arbitrary", "arbitrary"),
    )(page_table, q_lat, q_rope, new_lat, new_rope, w_uv, cache_lat, cache_rope_t)


def _diff_sample_kernel(pt_ref, lam_ref, q_ref, nk_ref, nv_ref, bias_ref, nbias_ref, g_ref, k_hbm, v_hbm, o_ref,
                        s_ref, vs_ref, k_buf, v_buf, sem, *, n_pages, t_new, out_scale, li, n_seq, n_steps):
    j = pl.program_id(1)
    q = q_ref[0]
    n_past = n_pages * n_steps
    slot = _page_gather(pt_ref, k_hbm, v_hbm, k_buf, v_buf, sem, li, n_pages, n_seq, n_steps)
    for n in range(n_pages):
        bias = bias_ref[:, n * PAGE_SIZE:(n + 1) * PAGE_SIZE]
        vs_ref[j * n_pages + n] = v_buf[slot, n].astype(BF16)
        s_ref[j * n_pages + n] = _dot_nt(q, k_buf[slot, n].astype(BF16)) + jnp.concatenate([bias, bias], axis=0)

    @pl.when(j == pl.num_programs(1) - 1)
    def _():
        nbias = nbias_ref[...]
        vs_ref[n_past] = nv_ref[0].astype(BF16)
        s_new = _dot_nt(q, nk_ref[0].astype(BF16)) + jnp.concatenate([nbias, nbias], axis=0)
        s_ref[n_past] = jnp.where(_new_token_mask(q.shape[0], t_new), s_new, NEG_INF)
        a = _softmax_pv(s_ref, vs_ref, n_past + 1)
        half = H_B * t_new
        o = _rms(a[:half] - lam_ref[0] * a[half:], g_ref[...]) * out_scale
        for h in range(H_B):
            o_ref[0, :, h * DV_B:(h + 1) * DV_B] = o[h * t_new:(h + 1) * t_new].astype(o_ref.dtype)


def _diff_sample(page_table, lam, q_pad, new_k, new_v, bias, new_bias, g_out, out_scale, cache_k, cache_v, li):
    b, rows, _ = q_pad.shape
    t_new = rows // (2 * H_B)
    n_pages = PAGES_PER_STEP
    total = page_table.shape[1]
    steps = total // n_pages
    per_seq = lambda r, w: pl.BlockSpec((1, r, w), lambda bi, j, pt, lm: (bi, 0, 0))
    hbm = pl.BlockSpec(memory_space=pl.ANY)
    return pl.pallas_call(
        functools.partial(_diff_sample_kernel, n_pages=n_pages, t_new=t_new, out_scale=out_scale, li=li, n_seq=b,
                          n_steps=steps),
        grid_spec=pltpu.PrefetchScalarGridSpec(
            num_scalar_prefetch=2,
            grid=(b, steps),
            in_specs=[per_seq(rows, 2 * DK_B), per_seq(PAGE_SIZE, 2 * DK_B), per_seq(PAGE_SIZE, DV_B),
                      pl.BlockSpec((rows // 2, n_pages * PAGE_SIZE), lambda bi, j, pt, lm: (0, j)),
                      pl.BlockSpec((rows // 2, PAGE_SIZE), lambda bi, j, pt, lm: (0, 0)),
                      pl.BlockSpec((1, DV_B), lambda bi, j, pt, lm: (0, 0)), hbm, hbm],
            out_specs=pl.BlockSpec((1, t_new, BRANCH_W), lambda bi, j, pt, lm: (bi, 0, 0)),
            scratch_shapes=[pltpu.VMEM((total + 1, rows, PAGE_SIZE), F32),
                            pltpu.VMEM((total + 1, PAGE_SIZE, DV_B), BF16),
                            pltpu.VMEM((PAGE_SLOTS, n_pages, PAGE_SIZE, 2 * DK_B), F32),
                            pltpu.VMEM((PAGE_SLOTS, n_pages, PAGE_SIZE, DV_B), F32),
                            pltpu.SemaphoreType.DMA((2, PAGE_SLOTS))]),
        out_shape=jax.ShapeDtypeStruct((b, t_new, BRANCH_W), F32),
        compiler_params=_params("arbitrary", "arbitrary"),
    )(page_table, lam, q_pad, new_k, new_v, bias, new_bias, g_out, cache_k, cache_v)


def _cumsum_rows(x):
    n = x.shape[0]
    row = lax.broadcasted_iota(jnp.int32, x.shape, 0)
    sh = 1
    while sh < n:
        x = x + jnp.where(row >= sh, pltpu.roll(x, sh, 0), 0.0)
        sh *= 2
    return x


def _group_row(x, group, j):
    n, w = x.shape
    if group == n:
        return jnp.broadcast_to(x[j:j + 1, :], (n, w))
    x3 = x.reshape(n // group, group, w)
    return jnp.broadcast_to(x3[:, j:j + 1, :], x3.shape).reshape(n, w)


def _hgrn_gates(cq, cf, a_lb, c_lb, oml):
    q = cq * jax.nn.sigmoid(cq)
    log_sig = jnp.minimum(cf, 0.0) - jnp.log1p(jnp.exp(-jnp.abs(cf)))
    x = c_lb + log_sig
    g = jnp.maximum(a_lb, x) + jnp.log1p(jnp.exp(-jnp.abs(a_lb - x)))
    k = oml * jax.nn.sigmoid(-cf)
    return q, k, g


def _hgrn_tile(q, k, v, g, st, sub):
    n = q.shape[0]
    b = _cumsum_rows(g)
    row = lax.broadcasted_iota(jnp.int32, (n, 1), 0)
    o = _dot_nt((q * jnp.exp(b)).astype(BF16), st.astype(BF16))
    for j in range(sub):
        e = jnp.where((row % sub) >= j, b - _group_row(b, sub, j), NEG_INF)
        r = jnp.sum(q * _group_row(k, sub, j) * jnp.exp(e), axis=-1, keepdims=True)
        o = o + r * _group_row(v, sub, j)
    half = sub
    while half < n:
        upper = (row % (2 * half)) >= half
        bb = _group_row(b, 2 * half, half - 1)
        qe = q * jnp.exp(jnp.where(upper, b - bb, NEG_INF))
        ke = k * jnp.exp(jnp.where(upper, NEG_INF, bb - b))
        s = _dot_nt(qe.astype(BF16), ke.astype(BF16))
        trow = lax.broadcasted_iota(jnp.int32, (n, n), 0) // (2 * half)
        tcol = lax.broadcasted_iota(jnp.int32, (n, n), 1) // (2 * half)
        o = o + _dot(jnp.where(trow == tcol, s, 0.0).astype(BF16), v.astype(BF16))
        half *= 2
    b_last = b[n - 1:n, :]
    kd = k * jnp.exp(b_last - b)
    st_next = st * jnp.exp(b_last) + _dot_tn(v.astype(BF16), kd.astype(BF16))
    return o, st_next


def _hgrn_prompt_kernel(cq_ref, cf_ref, ci_ref, cg_ref, a_ref, c_ref, oml_ref, go_ref, o_ref, s_ref, st_ref):
    t = pl.program_id(0)

    @pl.when(t == 0)
    def _():
        st_ref[...] = jnp.zeros(st_ref.shape, F32)

    for h in range(H_C):
        hl = slice(h * DK_C, (h + 1) * DK_C)
        q, k, g = _hgrn_gates(cq_ref[:, hl], cf_ref[:, hl], a_ref[:, hl], c_ref[:, hl], oml_ref[:, hl])
        o, st_next = _hgrn_tile(q, k, ci_ref[:, hl], g, st_ref[h], SUB)
        st_ref[h] = st_next
        o_ref[:, hl] = (_rms(o, go_ref[...]) * jax.nn.sigmoid(cg_ref[:, hl])).astype(o_ref.dtype)

        @pl.when(t == pl.num_programs(0) - 1)
        def _():
            s_ref[h] = st_next.T


def _hgrn_prompt(z_c, a_lb, c_lb, oml, g_out, t_len):
    tt = T_REC
    w = H_C * DK_C
    sect = lambda s: pl.BlockSpec((tt, w), functools.partial(lambda t, s: (t, s), s=s))
    vec = pl.BlockSpec((1, w), lambda t: (0, 0))
    return pl.pallas_call(
        _hgrn_prompt_kernel,
        grid=(t_len // tt,),
        in_specs=[sect(0), sect(1), sect(2), sect(3), vec, vec, vec, pl.BlockSpec((1, DV_C), lambda t: (0, 0))],
        out_specs=[pl.BlockSpec((tt, BRANCH_W), lambda t: (t, 0)),
                   pl.BlockSpec((H_C, DK_C, DV_C), lambda t: (0, 0, 0))],
        out_shape=[jax.ShapeDtypeStruct((t_len, BRANCH_W), BF16),
                   jax.ShapeDtypeStruct((H_C, DK_C, DV_C), F32)],
        scratch_shapes=[pltpu.VMEM((H_C, DV_C, DK_C), F32)],
        compiler_params=_params("arbitrary"),
    )(z_c, z_c, z_c, z_c, a_lb, c_lb, oml, g_out)


def _pad_rows(x, n):
    return jnp.concatenate([x, jnp.zeros((n - x.shape[0], x.shape[1]), x.dtype)], axis=0)


def _hgrn_sample_kernel(z_ref, s0_ref, a_ref, c_ref, oml_ref, go_ref, o_ref, s_ref):
    n = z_ref.shape[1]
    w = H_C * DK_C
    row = lax.broadcasted_iota(jnp.int32, (n, 1), 0)
    for sq in range(z_ref.shape[0]):
        z = z_ref[sq]
        for h in range(H_C):
            lanes = lambda s: slice(s * w + h * DK_C, s * w + (h + 1) * DK_C)
            hl = slice(h * DK_C, (h + 1) * DK_C)
            q, k, g = _hgrn_gates(z[:, lanes(0)], z[:, lanes(1)], a_ref[:, hl], c_ref[:, hl], oml_ref[:, hl])
            v = z[:, lanes(2)]
            s0 = s0_ref[0, sq, h]
            b = _cumsum_rows(g)
            o = _dot((q * jnp.exp(b)).astype(BF16), s0.astype(BF16))
            for j in range(n):
                e = jnp.where(row >= j, b - b[j:j + 1, :], NEG_INF)
                r = jnp.sum(q * k[j:j + 1, :] * jnp.exp(e), axis=-1, keepdims=True)
                o = o + r * v[j:j + 1, :]
            b_last = b[n - 1:n, :]
            kd = _pad_rows((k * jnp.exp(b_last - b)).astype(BF16), LANES)
            st_next = s0.T * jnp.exp(b_last) + _dot_tn(_pad_rows(v.astype(BF16), LANES), kd)
            s_ref[sq, h] = st_next.T
            o_ref[sq, :, hl] = (_rms(o, go_ref[...]) * jax.nn.sigmoid(z[:, lanes(3)])).astype(o_ref.dtype)


def _hgrn_sample(z_c, state, li, a_lb, c_lb, oml, g_out):
    b, t_new, w = z_c.shape
    ns = SEQS_PER_STEP
    assert b % ns == 0
    vec = pl.BlockSpec((1, H_C * DK_C), lambda i: (0, 0))
    return pl.pallas_call(
        _hgrn_sample_kernel,
        grid=(b // ns,),
        in_specs=[pl.BlockSpec((ns, t_new, w), lambda i: (i, 0, 0)),
                  pl.BlockSpec((1, ns, H_C, DK_C, DV_C), lambda i: (li, i, 0, 0, 0)),
                  vec, vec, vec, pl.BlockSpec((1, DV_C), lambda i: (0, 0))],
        out_specs=[pl.BlockSpec((ns, t_new, BRANCH_W), lambda i: (i, 0, 0)),
                   pl.BlockSpec((ns, H_C, DK_C, DV_C), lambda i: (i, 0, 0, 0))],
        out_shape=[jax.ShapeDtypeStruct((b, t_new, BRANCH_W), F32),
                   jax.ShapeDtypeStruct((b, H_C, DK_C, DV_C), F32)],
        compiler_params=_params("parallel"),
    )(z_c, state, a_lb, c_lb, oml, g_out)


def _ret_log_decay():
    return np.log1p(-np.exp2(-5.0 - np.arange(H_D, dtype=np.float64)))


def _ret_tables(n, block):
    lg = _ret_log_decay()
    t = np.arange(n)
    loc = t % block
    same = (t[:, None] // block) == (t[None, :] // block)
    diff = loc[:, None] - loc[None, :]
    dmat = np.where(same & (diff >= 0), np.exp(lg[:, None, None] * np.maximum(diff, 0)[None]), 0.0)
    per_lane = lambda f: np.repeat(f, DK_D, axis=1)
    dq = per_lane(np.exp(lg[None, :] * (loc[:, None] + 1)))
    dk = per_lane(np.exp(lg[None, :] * (block - 1 - loc[:, None])))
    dend = per_lane(np.exp(lg[None, :] * block))
    return tuple(jnp.asarray(a, F32) for a in (dmat, dq, dk, dend))


def _ret_prompt_kernel(rq_ref, rk_ref, rv_ref, rg_ref, cos_ref, sin_ref, dmat_ref, dq_ref, dk_ref, dend_ref, go_ref,
                       o_ref, s_ref, st_ref):
    t = pl.program_id(0)

    @pl.when(t == 0)
    def _():
        st_ref[...] = jnp.zeros(st_ref.shape, F32)

    cos, sin = cos_ref[...], sin_ref[...]
    q = _rope64(rq_ref[...], cos, sin)
    k = _rope64(rk_ref[...], cos, sin) * (DK_D ** -0.5)
    lane = lax.broadcasted_iota(jnp.int32, q.shape, 1)
    k_bf = k.astype(BF16)
    qd = q * dq_ref[...]
    kd = k * dk_ref[...]
    for h in range(H_D):
        in_head = (lane // DK_D) == h
        vl = slice(h * DV_D, (h + 1) * DV_D)
        v = rv_ref[:, vl].astype(BF16)
        st = st_ref[h]
        s = _dot_nt(jnp.where(in_head, q, 0.0).astype(BF16), k_bf) * dmat_ref[h]
        o = _dot(s.astype(BF16), v) + _dot_nt(jnp.where(in_head, qd, 0.0).astype(BF16), st.astype(BF16))
        st_next = st * dend_ref[...] + _dot_tn(v, jnp.where(in_head, kd, 0.0).astype(BF16))
        st_ref[h] = st_next
        o_ref[:, vl] = (_rms(o, go_ref[...]) * (rg_ref[:, vl] * jax.nn.sigmoid(rg_ref[:, vl]))).astype(o_ref.dtype)

        @pl.when(t == pl.num_programs(0) - 1)
        def _():
            s_ref[h] = st_next.T[h * DK_D:(h + 1) * DK_D, :]


def _ret_prompt(z_d, cos, sin, g_out, t_len):
    tt = T_RET
    wq = H_D * DK_D
    wv = H_D * DV_D
    dmat, dq, dk, dend = _ret_tables(tt, tt)
    const = lambda *shape: pl.BlockSpec(shape, lambda t: (0,) * len(shape))
    return pl.pallas_call(
        _ret_prompt_kernel,
        grid=(t_len // tt,),
        in_specs=[pl.BlockSpec((tt, wq), lambda t: (t, 0)), pl.BlockSpec((tt, wq), lambda t: (t, 1)),
                  pl.BlockSpec((tt, wv), lambda t: (t, 1)), pl.BlockSpec((tt, wv), lambda t: (t, 2)),
                  pl.BlockSpec((tt, wq), lambda t: (t, 0)), pl.BlockSpec((tt, wq), lambda t: (t, 0)),
                  const(H_D, tt, tt), const(tt, wq), const(tt, wq), const(1, wq), const(1, DV_D)],
        out_specs=[pl.BlockSpec((tt, BRANCH_W), lambda t: (t, 0)),
                   pl.BlockSpec((H_D, DK_D, DV_D), lambda t: (0, 0, 0))],
        out_shape=[jax.ShapeDtypeStruct((t_len, BRANCH_W), BF16),
                   jax.ShapeDtypeStruct((H_D, DK_D, DV_D), F32)],
        scratch_shapes=[pltpu.VMEM((H_D, DV_D, wq), F32)],
        compiler_params=_params("arbitrary"),
    )(z_d, z_d, z_d, z_d, cos, sin, dmat, dq, dk, dend, g_out)


def _ret_sample_kernel(z_ref, s0_ref, cos_ref, sin_ref, dmat_ref, dq_ref, dk_ref, dend_ref, go_ref, o_ref, s_ref):
    wq = H_D * DK_D
    cos, sin = cos_ref[...], sin_ref[...]
    for sq in range(z_ref.shape[0]):
        z = z_ref[sq]
        q = _rope64(z[:, :wq], cos, sin)
        k = _rope64(z[:, wq:2 * wq], cos, sin) * (DK_D ** -0.5)
        lane = lax.broadcasted_iota(jnp.int32, q.shape, 1)
        k_pad = _pad_rows(k.astype(BF16), LANES)
        qd = q * dq_ref[...]
        kd = _pad_rows((k * dk_ref[...]).astype(BF16), LANES)
        lane_pad = lax.broadcasted_iota(jnp.int32, kd.shape, 1)
        s_all = s0_ref[0, sq].reshape(wq, DV_D)
        for h in range(H_D):
            in_head = (lane // DK_D) == h
            rows = slice(h * DK_D, (h + 1) * DK_D)
            vl = slice(2 * wq + h * DV_D, 2 * wq + (h + 1) * DV_D)
            gl = slice(2 * wq + H_D * DV_D + h * DV_D, 2 * wq + H_D * DV_D + (h + 1) * DV_D)
            v_pad = _pad_rows(z[:, vl].astype(BF16), LANES)
            s = _dot_nt(jnp.where(in_head, q, 0.0).astype(BF16), k_pad) * dmat_ref[h]
            o = _dot(s.astype(BF16), v_pad) + _dot(jnp.where(in_head, qd, 0.0).astype(BF16), s_all.astype(BF16))
            upd = _dot_tn(jnp.where((lane_pad // DK_D) == h, kd, jnp.zeros_like(kd)), v_pad)
            s_ref[sq, h] = s_all[rows] * dend_ref[rows] + upd[rows]
            o_ref[sq, :, h * DV_D:(h + 1) * DV_D] = (
                _rms(o, go_ref[...]) * (z[:, gl] * jax.nn.sigmoid(z[:, gl]))).astype(o_ref.dtype)


def _ret_sample(z_d, state, li, cos, sin, g_out):
    b, t_new, w = z_d.shape
    wq = H_D * DK_D
    dmat, dq, dk, dend = _ret_tables(t_new, t_new)
    dmat = jnp.pad(dmat, ((0, 0), (0, 0), (0, LANES - t_new)))
    const = lambda *shape: pl.BlockSpec(shape, lambda i: (0,) * len(shape))
    ns = SEQS_PER_STEP
    assert b % ns == 0
    return pl.pallas_call(
        _ret_sample_kernel,
        grid=(b // ns,),
        in_specs=[pl.BlockSpec((ns, t_new, w), lambda i: (i, 0, 0)),
                  pl.BlockSpec((1, ns, H_D, DK_D, DV_D), lambda i: (li, i, 0, 0, 0)),
                  const(t_new, wq), const(t_new, wq), const(H_D, t_new, LANES), const(t_new, wq), const(t_new, wq),
                  const(wq, 1), const(1, DV_D)],
        out_specs=[pl.BlockSpec((ns, t_new, BRANCH_W), lambda i: (i, 0, 0)),
                   pl.BlockSpec((ns, H_D, DK_D, DV_D), lambda i: (i, 0, 0, 0))],
        out_shape=[jax.ShapeDtypeStruct((b, t_new, BRANCH_W), F32),
                   jax.ShapeDtypeStruct((b, H_D, DK_D, DV_D), F32)],
        compiler_params=_params("parallel"),
    )(z_d, state, cos, sin, dmat, dq, dk, dend.reshape(wq, 1), g_out)


def _t5_bucket(rel):
    n = jnp.maximum(jnp.asarray(rel, jnp.int32), 0)
    large = T5_MAX_EXACT + (jnp.log(jnp.maximum(n, 1).astype(F32) / T5_MAX_EXACT)
                            / math.log(T5_MAX_DIST / T5_MAX_EXACT)
                            * (N_BUCKETS - T5_MAX_EXACT)).astype(jnp.int32)
    return jnp.where(n < T5_MAX_EXACT, n, jnp.minimum(large, N_BUCKETS - 1))


def _bias_lookup(table, buckets):
    out = jnp.zeros((table.shape[1],) + buckets.shape, F32)
    expand = (slice(None),) + (None,) * buckets.ndim
    for b in range(N_BUCKETS):
        out = jnp.where(buckets[None] == b, table[b][expand], out)
    return out


def _layer_weights(li, w_in, mla_w_qb, mla_w_uk, mla_w_uv, w_branch, w_out, w_gate, w_up, w_down):
    offs = np.cumsum([0, Q_LORA, KV_LORA, D_ROPE, H_B * 2 * DK_B, 2 * DK_B, DV_B, H_C * DK_C, H_C * DK_C, H_C * DV_C,
                      H_C * DV_C, H_D * DK_D, H_D * DK_D, H_D * DV_D, H_D * DV_D, N_BRANCH * D_MODEL])
    w = w_in[li]
    col = lambda a, b: w[:, offs[a]:offs[b]]
    w_a = jnp.concatenate([col(0, 2)] + [col(2, 3)] * H_A, axis=1)
    groups = [w_a, col(3, 6), col(6, 10), col(10, 14), col(14, 15)]
    wqb = mla_w_qb[li]
    return {
        "w_in": [g.astype(BF16) for g in groups],
        "w_nope": jnp.transpose(wqb[:, :, :D_NOPE], (1, 0, 2)).astype(BF16),
        "w_rope": wqb[:, :, D_NOPE:].reshape(Q_LORA, H_A * D_ROPE).astype(BF16),
        "w_ukT": jnp.transpose(mla_w_uk[li], (1, 2, 0)).astype(BF16),
        "w_uv": jnp.transpose(mla_w_uv[li], (1, 0, 2)).astype(BF16),
        "w_branch": w_branch[li].astype(BF16),
        "w_out": w_out[li].astype(BF16),
        "w_gate": w_gate[li].astype(BF16),
        "w_up": w_up[li].astype(BF16),
        "w_down": w_down[li].astype(BF16),
    }


def kernel(x_prompt, x_sample, cache_mla_latent, cache_mla_rope, cache_diff_k, cache_diff_v, state_hgrn, state_ret, page_table, t5_bias, norm_attn, w_in, mla_q_norm, mla_w_qb, mla_q_head_norm, mla_kv_norm, mla_krope_norm, mla_w_uk, mla_w_uv, diff_q_norm, diff_k_norm, diff_lambda, diff_out_norm, hgrn_lb_logits, hgrn_out_norm, ret_out_norm, w_branch, w_out, norm_ffn, w_gate, w_up, w_down):
    n_batch, t_p, _ = x_prompt.shape
    n_seq, t_s, _ = x_sample.shape
    assert n_batch == 1 and t_p % T_ATTN == 0
    past = page_table.shape[1] * PAGE_SIZE
    n_s = n_seq * t_s
    assert (t_p + n_s) % TM_DENSE == 0 and page_table.shape[1] % PAGES_PER_STEP == 0

    x = jnp.concatenate([x_prompt.reshape(t_p, D_MODEL), x_sample.reshape(n_s, D_MODEL)], axis=0)
    pos = jnp.concatenate([jnp.arange(t_p, dtype=jnp.int32), jnp.tile(past + jnp.arange(t_s, dtype=jnp.int32), n_seq)])
    cos, sin = _rope_tables(pos, H_A * D_ROPE)
    cos_s, sin_s = cos[t_p:t_p + t_s], sin[t_p:t_p + t_s]

    ta = np.arange(T_ATTN)
    rel = ta[:, None] - ta[None, :]
    bias_table = t5_bias.astype(F32) * LOG2E
    patch_buckets = jnp.stack([_t5_bucket(rel + T_ATTN), _t5_bucket(rel)])
    bias_patches = jnp.transpose(_bias_lookup(bias_table, patch_buckets), (1, 0, 2, 3))
    bias_far = _bias_lookup(bias_table, _t5_bucket(np.full((1,), 2 * T_ATTN))).reshape(H_B)
    q_pos = past + np.arange(t_s)
    strip_buckets = _t5_bucket(q_pos[:, None] - np.arange(past + PAGE_SIZE)[None, :])
    bias_strip = _bias_lookup(bias_table, strip_buckets).reshape(H_B * t_s, past + PAGE_SIZE)
    bias_past, bias_new = bias_strip[:, :past], bias_strip[:, past:]

    lb_cum = jnp.cumsum(jax.nn.softmax(hgrn_lb_logits.astype(F32), axis=0), axis=0)
    lb_all = lb_cum - lb_cum[:1]

    cache_rope_t = jnp.swapaxes(cache_mla_rope, 2, 3)
    cache_k = cache_diff_k.reshape(cache_diff_k.shape[:3] + (2 * DK_B,))
    cache_v = cache_diff_v.reshape(cache_diff_v.shape[:3] + (DV_B,))
    pad_new = lambda a: jnp.pad(a.reshape(n_seq, t_s, a.shape[-1]), ((0, 0), (0, PAGE_SIZE - t_s), (0, 0)))

    outs_p, outs_s = [], []
    for li in range(DEPTH):
        wts = _layer_weights(li, w_in, mla_w_qb, mla_w_uk, mla_w_uv, w_branch, w_out, w_gate, w_up, w_down)
        z_a, z_b, z_c, z_d, z_g = [
            _norm_matmul(x, norm_attn[li], w, tn, dt)
            for w, tn, dt in zip(wts["w_in"],
                                 (wts["w_in"][0].shape[1], wts["w_in"][1].shape[1], 2 * TN_DENSE, TN_DENSE, 2 * TN_DENSE),
                                 (F32, F32, F32, F32, BF16))]

        pa = {"g_q": mla_q_norm[li].reshape(1, -1), "w_nope": wts["w_nope"], "w_rope": wts["w_rope"],
              "g_hn": mla_q_head_norm[li, :D_NOPE].reshape(1, -1),
              "g_hr": jnp.tile(mla_q_head_norm[li, D_NOPE:], H_A).reshape(1, -1), "w_ukT": wts["w_ukT"],
              "g_kv": mla_kv_norm[li].reshape(1, -1), "g_kr": jnp.tile(mla_krope_norm[li], H_A).reshape(1, -1)}
        qcat, kcat, c_kv, k_rope = _mla_prep(z_a, pa, cos, sin)
        o_a_p = _mla_prompt(qcat, kcat, wts["w_uv"], t_p)
        qs = qcat[:, t_p:].reshape(H_A, n_seq, t_s, -1)
        q_lat_s = jnp.transpose(qs[..., :KV_LORA], (1, 0, 2, 3)).reshape(n_seq, H_A * t_s, KV_LORA)
        q_rope_s = jnp.stack([qs[h, :, :, KV_LORA + h * D_ROPE:KV_LORA + (h + 1) * D_ROPE] for h in range(H_A)], axis=1)
        q_rope_s = q_rope_s.reshape(n_seq, H_A * t_s, D_ROPE)
        o_a_s = _mla_sample(page_table, q_lat_s, q_rope_s, pad_new(c_kv[t_p:]), pad_new(k_rope[t_p:]), wts["w_uv"],
                            cache_mla_latent, cache_rope_t, li)

        lam_init = 0.8 - 0.6 * math.exp(-0.3 * li)
        lv = diff_lambda[li].astype(F32)
        lam = (jnp.exp(jnp.sum(lv[0] * lv[1])) - jnp.exp(jnp.sum(lv[2] * lv[3])) + lam_init).reshape(1)
        qb, kb, vb = _diff_prep(z_b, jnp.tile(diff_q_norm[li], 2 * H_B).reshape(1, -1),
                                jnp.tile(diff_k_norm[li], 2).reshape(1, -1))
        g_diff = diff_out_norm[li].reshape(1, -1)
        o_b_p = _diff_prompt(qb, kb.astype(BF16), vb.astype(BF16), bias_patches, bias_far, lam, g_diff,
                             1.0 - lam_init, t_p)
        qbs = qb[t_p:].reshape(n_seq, t_s, H_B, 2, DK_B)
        zero = jnp.zeros_like(qbs[..., 0, :])
        q_pad = jnp.stack([jnp.concatenate([qbs[..., 0, :], zero], axis=-1),
                           jnp.concatenate([zero, qbs[..., 1, :]], axis=-1)], axis=1)
        q_pad = jnp.transpose(q_pad, (0, 1, 3, 2, 4)).reshape(n_seq, 2 * H_B * t_s, 2 * DK_B)
        o_b_s = _diff_sample(page_table, lam, q_pad, pad_new(kb[t_p:]), pad_new(vb[t_p:]), bias_past, bias_new, g_diff,
                             1.0 - lam_init, cache_k, cache_v, li)

        lb = lb_all[li].reshape(1, -1)
        a_lb, c_lb, oml = jnp.log(lb), jnp.log1p(-lb), 1.0 - lb
        g_hgrn = hgrn_out_norm[li].reshape(1, -1)
        o_c_p, s_hgrn_p = _hgrn_prompt(z_c, a_lb, c_lb, oml, g_hgrn, t_p)
        o_c_s, s_hgrn_s = _hgrn_sample(z_c[t_p:].reshape(n_seq, t_s, -1), state_hgrn, li, a_lb, c_lb, oml, g_hgrn)

        g_ret = ret_out_norm[li].reshape(1, -1)
        o_d_p, s_ret_p = _ret_prompt(z_d, cos, sin, g_ret, t_p)
        o_d_s, s_ret_s = _ret_sample(z_d[t_p:].reshape(n_seq, t_s, -1), state_ret, li, cos_s, sin_s, g_ret)

        mixed = _merge([o_a_p, o_b_p, o_c_p, o_d_p],
                       [o.reshape(n_s, BRANCH_W) for o in (o_a_s, o_b_s, o_c_s, o_d_s)], wts["w_branch"], z_g)
        x = _matmul_res(mixed, wts["w_out"], x, TM_DENSE, 2 * TN_DENSE)
        u = _ffn_up(x, norm_ffn[li], wts["w_gate"], wts["w_up"])
        x = _matmul_res(u, wts["w_down"], x, TM_DENSE, TN_DENSE)

        outs_p.append((c_kv[:t_p].reshape(1, t_p, KV_LORA), k_rope[:t_p].reshape(1, t_p, D_ROPE),
                       kb[:t_p].reshape(1, t_p, 1, 2 * DK_B), vb[:t_p].reshape(1, t_p, 1, DV_B),
                       s_hgrn_p[None], s_ret_p[None]))
        outs_s.append((c_kv[t_p:].reshape(n_seq, t_s, KV_LORA), k_rope[t_p:].reshape(n_seq, t_s, D_ROPE),
                       kb[t_p:].reshape(n_seq, t_s, 1, 2 * DK_B), vb[t_p:].reshape(n_seq, t_s, 1, DV_B),
                       s_hgrn_s, s_ret_s))

    stack = lambda outs: [jnp.stack(a) for a in zip(*outs)]
    y_p = x[:t_p].reshape(1, t_p, D_MODEL)
    y_s = x[t_p:].reshape(n_seq, t_s, D_MODEL)
    return (y_p, y_s, *stack(outs_p), *stack(outs_s))
```

```python
import functools
import math

import numpy as np
import jax
import jax.numpy as jnp
from jax import lax
from jax.experimental import pallas as pl
from jax.experimental.pallas import tpu as pltpu

F32 = jnp.float32
BF16 = jnp.bfloat16

D_MODEL = 2048
DEPTH = 2
PAGE_SIZE = 128
N_BRANCH = 4
BRANCH_W = D_MODEL // 4
H_A, D_NOPE, D_ROPE, DV_A = 4, 128, 64, 128
Q_LORA, KV_LORA = 384, 256
MLA_SCALE = (D_NOPE + D_ROPE) ** -0.5
H_B, DK_B, DV_B = 4, 64, 128
DIFF_SCALE = DK_B ** -0.5
H_C, DK_C, DV_C = 4, 128, 128
H_D, DK_D, DV_D = 4, 64, 128
D_FF = 5632
N_BUCKETS, T5_MAX_EXACT, T5_MAX_DIST = 32, 16, 128
ROPE_THETA = 10000.0
EPS = 1e-6
LOG2E = math.log2(math.e)
MLA_QSCALE = MLA_SCALE * LOG2E
DIFF_QSCALE = DIFF_SCALE * LOG2E

LANES = 128
VMEM_LIMIT_BYTES = 56 * 2 ** 20

TM_DENSE = 1024
TN_DENSE = 512
TM_PREP = 256
T_ATTN = 512
T_REC = 128
SUB = 8
PAGES_PER_STEP = 32
MLA_PAGES_PER_STEP = 64
PAGE_SLOTS = 3
SEQS_PER_STEP = 4
T_RET = 512
NEG_INF = float("-inf")


def _params(*sem):
    return pltpu.CompilerParams(dimension_semantics=sem, vmem_limit_bytes=VMEM_LIMIT_BYTES)


def _rms(x, gain):
    return x * lax.rsqrt(jnp.mean(x * x, axis=-1, keepdims=True) + EPS) * gain


def _dot(a, b):
    return jnp.dot(a, b, preferred_element_type=F32)


def _dot_nt(a, b):
    return lax.dot_general(a, b, (((1,), (1,)), ((), ())), preferred_element_type=F32)


def _dot_tn(a, b):
    return lax.dot_general(a, b, (((0,), (0,)), ((), ())), preferred_element_type=F32)


def _norm_matmul_kernel(x_ref, g_ref, w_ref, o_ref, h_ref):
    @pl.when(pl.program_id(1) == 0)
    def _():
        h_ref[...] = _rms(x_ref[...], g_ref[...]).astype(BF16)

    o_ref[...] = _dot(h_ref[...], w_ref[...]).astype(o_ref.dtype)


def _norm_matmul(x, gain, w, tn, out_dtype=F32):
    m, k = x.shape
    n = w.shape[1]
    tm = min(TM_DENSE, m)
    assert m % tm == 0 and n % tn == 0
    return pl.pallas_call(
        _norm_matmul_kernel,
        grid=(m // tm, n // tn),
        in_specs=[pl.BlockSpec((tm, k), lambda i, j: (i, 0)),
                  pl.BlockSpec((1, k), lambda i, j: (0, 0)),
                  pl.BlockSpec((k, tn), lambda i, j: (0, j))],
        out_specs=pl.BlockSpec((tm, tn), lambda i, j: (i, j)),
        out_shape=jax.ShapeDtypeStruct((m, n), out_dtype),
        scratch_shapes=[pltpu.VMEM((tm, k), BF16)],
        compiler_params=_params("parallel", "arbitrary"),
    )(x, gain.reshape(1, k), w)


def _ffn_up_kernel(x_ref, g_ref, wg_ref, wu_ref, o_ref, h_ref):
    @pl.when(pl.program_id(1) == 0)
    def _():
        h_ref[...] = _rms(x_ref[...], g_ref[...]).astype(BF16)

    h = h_ref[...]
    a = _dot(h, wg_ref[...])
    o_ref[...] = (a * jax.nn.sigmoid(a) * _dot(h, wu_ref[...])).astype(o_ref.dtype)


def _ffn_up(x, gain, wg, wu):
    m, k = x.shape
    n = wg.shape[1]
    tm, tn = min(TM_DENSE, m), TN_DENSE
    assert m % tm == 0 and n % tn == 0
    return pl.pallas_call(
        _ffn_up_kernel,
        grid=(m // tm, n // tn),
        in_specs=[pl.BlockSpec((tm, k), lambda i, j: (i, 0)),
                  pl.BlockSpec((1, k), lambda i, j: (0, 0)),
                  pl.BlockSpec((k, tn), lambda i, j: (0, j)),
                  pl.BlockSpec((k, tn), lambda i, j: (0, j))],
        out_specs=pl.BlockSpec((tm, tn), lambda i, j: (i, j)),
        out_shape=jax.ShapeDtypeStruct((m, n), BF16),
        scratch_shapes=[pltpu.VMEM((tm, k), BF16)],
        compiler_params=_params("parallel", "arbitrary"),
    )(x, gain.reshape(1, k), wg, wu)


def _matmul_res_kernel(a_ref, w_ref, r_ref, o_ref):
    o_ref[...] = r_ref[...] + _dot(a_ref[...], w_ref[...])


def _matmul_res(a, w, res, tm, tn):
    m, k = a.shape
    n = w.shape[1]
    assert m % tm == 0 and n % tn == 0
    return pl.pallas_call(
        _matmul_res_kernel,
        grid=(m // tm, n // tn),
        in_specs=[pl.BlockSpec((tm, k), lambda i, j: (i, 0)),
                  pl.BlockSpec((k, tn), lambda i, j: (0, j)),
                  pl.BlockSpec((tm, tn), lambda i, j: (i, j))],
        out_specs=pl.BlockSpec((tm, tn), lambda i, j: (i, j)),
        out_shape=jax.ShapeDtypeStruct((m, n), F32),
        compiler_params=_params("parallel", "arbitrary"),
    )(a, w, res)


def _merge_kernel(*refs, prompt_tiles):
    p_refs, s_refs = refs[:N_BRANCH], refs[N_BRANCH:2 * N_BRANCH]
    wb_ref = refs[2 * N_BRANCH]
    g_refs = refs[2 * N_BRANCH + 1:3 * N_BRANCH + 1]
    o_ref = refs[3 * N_BRANCH + 1]
    i = pl.program_id(0)

    def merge(branch):
        acc = None
        for n in range(N_BRANCH):
            t = jax.nn.sigmoid(g_refs[n][...].astype(F32)) * _dot(branch(n), wb_ref[n])
            acc = t if acc is None else acc + t
        o_ref[...] = acc.astype(o_ref.dtype)

    @pl.when(i < prompt_tiles)
    def _():
        merge(lambda n: p_refs[n][...])

    @pl.when(i >= prompt_tiles)
    def _():
        merge(lambda n: s_refs[n][...].astype(BF16))


def _merge(prompt_branches, sample_branches, w_branch, gate_logits):
    t_p, n_s = prompt_branches[0].shape[0], sample_branches[0].shape[0]
    tm, tn = TM_DENSE, TN_DENSE
    assert t_p % tm == 0 and n_s % tm == 0
    pt, st = t_p // tm, n_s // tm
    nj = D_MODEL // tn
    pspec = pl.BlockSpec((tm, BRANCH_W), lambda i, j: (jnp.minimum(i, pt - 1), 0))
    sspec = pl.BlockSpec((tm, BRANCH_W), lambda i, j: (jnp.maximum(i - pt, 0), 0))
    gspecs = [pl.BlockSpec((tm, tn), functools.partial(lambda i, j, n: (i, n * nj + j), n=n))
              for n in range(N_BRANCH)]
    return pl.pallas_call(
        functools.partial(_merge_kernel, prompt_tiles=pt),
        grid=(pt + st, nj),
        in_specs=[pspec] * N_BRANCH + [sspec] * N_BRANCH
        + [pl.BlockSpec((N_BRANCH, BRANCH_W, tn), lambda i, j: (0, 0, j))] + gspecs,
        out_specs=pl.BlockSpec((tm, tn), lambda i, j: (i, j)),
        out_shape=jax.ShapeDtypeStruct((t_p + n_s, D_MODEL), BF16),
        compiler_params=_params("parallel", "arbitrary"),
    )(*prompt_branches, *sample_branches, w_branch, gate_logits, gate_logits, gate_logits, gate_logits)


def _rope64(x, cos, sin_signed):
    w = x.shape[-1]
    lane = lax.broadcasted_iota(jnp.int32, x.shape, x.ndim - 1)
    first_half = (lane % D_ROPE) < (D_ROPE // 2)
    partner = jnp.where(first_half, pltpu.roll(x, w - D_ROPE // 2, x.ndim - 1), pltpu.roll(x, D_ROPE // 2, x.ndim - 1))
    return x * cos + partner * sin_signed


def _rope_tables(pos, width):
    half = D_ROPE // 2
    inv = ROPE_THETA ** (-jnp.arange(half, dtype=F32) / half)
    ang = pos.astype(F32)[:, None] * inv[None, :]
    cos, sin = jnp.cos(ang), jnp.sin(ang)
    reps = width // D_ROPE
    return (jnp.tile(jnp.concatenate([cos, cos], axis=-1), (1, reps)),
            jnp.tile(jnp.concatenate([-sin, sin], axis=-1), (1, reps)))


def _mla_prep_kernel(z_ref, gq_ref, wn_ref, wr_ref, ghn_ref, ghr_ref, wuk_ref, gkv_ref, gkr_ref, cos_ref, sin_ref,
                     qcat_ref, kcat_ref, ckv_ref, krope_ref):
    z = z_ref[...]
    qn = _rms(z[:, :Q_LORA], gq_ref[...]).astype(BF16)
    qr = _dot(qn, wr_ref[...])
    lane = lax.broadcasted_iota(jnp.int32, qr.shape, 1)
    qr2 = qr * qr
    scale_lanes = jnp.zeros_like(qr)
    q_lat = []
    for h in range(H_A):
        in_head = (lane // D_ROPE) == h
        qh = _dot(qn, wn_ref[h])
        ss = jnp.sum(qh * qh, axis=-1, keepdims=True) + jnp.sum(jnp.where(in_head, qr2, 0.0), axis=-1, keepdims=True)
        r = lax.rsqrt(ss * (1.0 / (D_NOPE + D_ROPE)) + EPS)
        scale_lanes = jnp.where(in_head, r, scale_lanes)
        q_lat.append(_dot((qh * r * ghn_ref[...]).astype(BF16), wuk_ref[h]))
    cos, sin = cos_ref[...], sin_ref[...]
    qrr = _rope64(qr * scale_lanes * ghr_ref[...], cos, sin)
    for h in range(H_A):
        in_head = (lane // D_ROPE) == h
        qcat_ref[h] = jnp.concatenate([q_lat[h] * MLA_QSCALE, jnp.where(in_head, qrr, 0.0) * MLA_QSCALE],
                                      axis=-1).astype(BF16)
    ckv = _rms(z[:, Q_LORA:Q_LORA + KV_LORA], gkv_ref[...])
    krr = _rope64(_rms(z[:, Q_LORA + KV_LORA:], gkr_ref[...]), cos, sin)
    ckv_ref[...] = ckv
    krope_ref[...] = krr[:, :D_ROPE]
    kcat_ref[...] = jnp.concatenate([ckv, krr], axis=-1).astype(BF16)


def _mla_prep(z_a, p, cos, sin):
    m = z_a.shape[0]
    tm = TM_PREP
    wa = z_a.shape[1]
    wr = H_A * D_ROPE
    const = lambda *shape: pl.BlockSpec(shape, lambda i: (0,) * len(shape))
    return pl.pallas_call(
        _mla_prep_kernel,
        grid=(m // tm,),
        in_specs=[pl.BlockSpec((tm, wa), lambda i: (i, 0)),
                  const(1, Q_LORA), const(H_A, Q_LORA, D_NOPE), const(Q_LORA, wr), const(1, D_NOPE), const(1, wr),
                  const(H_A, D_NOPE, KV_LORA), const(1, KV_LORA), const(1, wr),
                  pl.BlockSpec((tm, wr), lambda i: (i, 0)), pl.BlockSpec((tm, wr), lambda i: (i, 0))],
        out_specs=[pl.BlockSpec((H_A, tm, KV_LORA + wr), lambda i: (0, i, 0)),
                   pl.BlockSpec((tm, KV_LORA + wr), lambda i: (i, 0)),
                   pl.BlockSpec((tm, KV_LORA), lambda i: (i, 0)),
                   pl.BlockSpec((tm, D_ROPE), lambda i: (i, 0))],
        out_shape=[jax.ShapeDtypeStruct((H_A, m, KV_LORA + wr), BF16),
                   jax.ShapeDtypeStruct((m, KV_LORA + wr), BF16),
                   jax.ShapeDtypeStruct((m, KV_LORA), F32),
                   jax.ShapeDtypeStruct((m, D_ROPE), F32)],
        compiler_params=_params("parallel"),
    )(z_a, p["g_q"], p["w_nope"], p["w_rope"], p["g_hn"], p["g_hr"], p["w_ukT"], p["g_kv"], p["g_kr"], cos, sin)


def _group_rms64(x, gain_tiled):
    lane = lax.broadcasted_iota(jnp.int32, x.shape, 1)
    x2 = x * x
    scale = jnp.zeros_like(x)
    for g in range(x.shape[1] // DK_B):
        in_g = (lane // DK_B) == g
        ss = jnp.sum(jnp.where(in_g, x2, 0.0), axis=-1, keepdims=True)
        scale = jnp.where(in_g, lax.rsqrt(ss * (1.0 / DK_B) + EPS), scale)
    return x * scale * gain_tiled


def _diff_prep_kernel(z_ref, gq_ref, gk_ref, q_ref, k_ref, v_ref):
    z = z_ref[...]
    nq = H_B * 2 * DK_B
    q_ref[...] = (_group_rms64(z[:, :nq], gq_ref[...]) * DIFF_QSCALE).astype(BF16)
    k_ref[...] = _group_rms64(z[:, nq:nq + 2 * DK_B], gk_ref[...])
    v_ref[...] = z[:, nq + 2 * DK_B:]


def _diff_prep(z_b, gq_tiled, gk_tiled):
    m, wb = z_b.shape
    tm = TM_PREP
    nq = H_B * 2 * DK_B
    return pl.pallas_call(
        _diff_prep_kernel,
        grid=(m // tm,),
        in_specs=[pl.BlockSpec((tm, wb), lambda i: (i, 0)),
                  pl.BlockSpec((1, nq), lambda i: (0, 0)), pl.BlockSpec((1, 2 * DK_B), lambda i: (0, 0))],
        out_specs=[pl.BlockSpec((tm, nq), lambda i: (i, 0)),
                   pl.BlockSpec((tm, 2 * DK_B), lambda i: (i, 0)),
                   pl.BlockSpec((tm, DV_B), lambda i: (i, 0))],
        out_shape=[jax.ShapeDtypeStruct((m, nq), BF16),
                   jax.ShapeDtypeStruct((m, 2 * DK_B), F32),
                   jax.ShapeDtypeStruct((m, DV_B), F32)],
        compiler_params=_params("parallel"),
    )(z_b, gq_tiled, gk_tiled)


def _widen(x, width):
    return x if width == LANES else jnp.concatenate([x] * (width // LANES), axis=1)


def _softmax_step(s, m_prev, l_prev, shift=None):
    m_cur = jnp.max(s, axis=-1, keepdims=True)
    if shift is not None:
        m_cur = m_cur + shift
    m_next = jnp.maximum(m_prev, m_cur)
    alpha = jnp.exp2(m_prev - m_next)
    p = jnp.exp2(s - _widen(m_next if shift is None else m_next - shift, s.shape[1]))
    l_next = alpha * l_prev + jnp.sum(p, axis=-1, keepdims=True)
    return p, m_next, l_next, alpha


def _init_softmax_state(m_ref, l_ref, acc_ref):
    m_ref[...] = jnp.full(m_ref.shape, NEG_INF, F32)
    l_ref[...] = jnp.zeros(l_ref.shape, F32)
    acc_ref[...] = jnp.zeros(acc_ref.shape, F32)


def _causal_tiles(n):
    pairs = [(i, j) for i in range(n) for j in range(i + 1)]
    return jnp.asarray([p[0] for p in pairs], jnp.int32), jnp.asarray([p[1] for p in pairs], jnp.int32)


def _mla_prompt_kernel(qi_ref, kj_ref, q_ref, k_ref, wuv_ref, o_ref, m_ref, l_ref, acc_ref):
    step = pl.program_id(0)
    i, j = qi_ref[step], kj_ref[step]
    t = k_ref.shape[0]

    @pl.when(j == 0)
    def _():
        _init_softmax_state(m_ref, l_ref, acc_ref)

    def tile(diagonal):
        k = k_ref[...]
        v = k[:, :KV_LORA]
        if diagonal:
            visible = lax.broadcasted_iota(jnp.int32, (t, t), 1) <= lax.broadcasted_iota(jnp.int32, (t, t), 0)
        for h in range(H_A):
            s = _dot_nt(q_ref[h], k)
            if diagonal:
                s = jnp.where(visible, s, NEG_INF)
            p, m_next, l_next, alpha = _softmax_step(s, m_ref[h], l_ref[h])
            m_ref[h] = m_next
            l_ref[h] = l_next
            acc_ref[h] = acc_ref[h] * _widen(alpha, KV_LORA) + _dot(p.astype(BF16), v)

    @pl.when(j < i)
    def _():
        tile(False)

    @pl.when(j == i)
    def _():
        tile(True)
        for h in range(H_A):
            o_lat = acc_ref[h] / _widen(l_ref[h], KV_LORA)
            o_ref[:, h * DV_A:(h + 1) * DV_A] = _dot(o_lat.astype(BF16), wuv_ref[h]).astype(o_ref.dtype)


def _mla_prompt(qcat, kcat, w_uv, t_len):
    t = T_ATTN
    qi, kj = _causal_tiles(t_len // t)
    dk = kcat.shape[1]
    return pl.pallas_call(
        _mla_prompt_kernel,
        grid_spec=pltpu.PrefetchScalarGridSpec(
            num_scalar_prefetch=2,
            grid=(qi.shape[0],),
            in_specs=[pl.BlockSpec((H_A, t, dk), lambda s, qi, kj: (0, qi[s], 0)),
                      pl.BlockSpec((t, dk), lambda s, qi, kj: (kj[s], 0)),
                      pl.BlockSpec((H_A, KV_LORA, DV_A), lambda s, qi, kj: (0, 0, 0))],
            out_specs=pl.BlockSpec((t, BRANCH_W), lambda s, qi, kj: (qi[s], 0)),
            scratch_shapes=[pltpu.VMEM((H_A, t, LANES), F32), pltpu.VMEM((H_A, t, LANES), F32),
                            pltpu.VMEM((H_A, t, KV_LORA), F32)]),
        out_shape=jax.ShapeDtypeStruct((t_len, BRANCH_W), BF16),
        compiler_params=_params("arbitrary"),
    )(qi, kj, qcat, kcat, w_uv)


def _diff_finish(acc0, l0, acc1, l1, lam, gain, out_scale):
    o = acc0 / l0 - lam * (acc1 / l1)
    return _rms(o, gain) * out_scale


def _diff_prompt_kernel(qi_ref, kj_ref, lam_ref, far_ref, q_ref, k_ref, v_ref, bias_ref, g_ref, o_ref,
                        m_ref, l_ref, acc_ref, *, out_scale):
    step = pl.program_id(0)
    i, j = qi_ref[step], kj_ref[step]
    t = k_ref.shape[0]

    @pl.when(j == 0)
    def _():
        _init_softmax_state(m_ref, l_ref, acc_ref)

    def tile(kind):
        k = k_ref[...]
        v = v_ref[...]
        lane = lax.broadcasted_iota(jnp.int32, (t, 2 * DK_B), 1)
        if kind == "diag":
            visible = lax.broadcasted_iota(jnp.int32, (t, t), 1) <= lax.broadcasted_iota(jnp.int32, (t, t), 0)
        for h in range(H_B):
            qh = q_ref[:, h * 2 * DK_B:(h + 1) * 2 * DK_B]
            for mp in range(2):
                qm = jnp.where((lane // DK_B) == mp, qh, jnp.zeros_like(qh))
                s = _dot_nt(qm, k)
                shift = None
                if kind == "far":
                    shift = far_ref[h]
                else:
                    s = s + bias_ref[0 if kind == "near" else 1, h]
                if kind == "diag":
                    s = jnp.where(visible, s, NEG_INF)
                idx = mp * H_B + h
                p, m_next, l_next, alpha = _softmax_step(s, m_ref[idx], l_ref[idx], shift)
                m_ref[idx] = m_next
                l_ref[idx] = l_next
                acc_ref[idx] = acc_ref[idx] * alpha + _dot(p.astype(BF16), v)

    @pl.when(j < i - 1)
    def _():
        tile("far")

    @pl.when(j == i - 1)
    def _():
        tile("near")

    @pl.when(j == i)
    def _():
        tile("diag")
        lam = lam_ref[0]
        for h in range(H_B):
            o = _diff_finish(acc_ref[h], l_ref[h], acc_ref[H_B + h], l_ref[H_B + h], lam, g_ref[...], out_scale)
            o_ref[:, h * DV_B:(h + 1) * DV_B] = o.astype(o_ref.dtype)


def _diff_prompt(qb, kb, vb, bias_patches, bias_far, lam, g_out, out_scale, t_len):
    t = T_ATTN
    assert t >= T5_MAX_DIST
    qi, kj = _causal_tiles(t_len // t)
    return pl.pallas_call(
        functools.partial(_diff_prompt_kernel, out_scale=out_scale),
        grid_spec=pltpu.PrefetchScalarGridSpec(
            num_scalar_prefetch=4,
            grid=(qi.shape[0],),
            in_specs=[pl.BlockSpec((t, H_B * 2 * DK_B), lambda s, qi, kj, lam, far: (qi[s], 0)),
                      pl.BlockSpec((t, 2 * DK_B), lambda s, qi, kj, lam, far: (kj[s], 0)),
                      pl.BlockSpec((t, DV_B), lambda s, qi, kj, lam, far: (kj[s], 0)),
                      pl.BlockSpec((2, H_B, t, t), lambda s, qi, kj, lam, far: (0, 0, 0, 0)),
                      pl.BlockSpec((1, DV_B), lambda s, qi, kj, lam, far: (0, 0))],
            out_specs=pl.BlockSpec((t, BRANCH_W), lambda s, qi, kj, lam, far: (qi[s], 0)),
            scratch_shapes=[pltpu.VMEM((2 * H_B, t, LANES), F32), pltpu.VMEM((2 * H_B, t, LANES), F32),
                            pltpu.VMEM((2 * H_B, t, DV_B), F32)]),
        out_shape=jax.ShapeDtypeStruct((t_len, BRANCH_W), BF16),
        compiler_params=_params("arbitrary"),
    )(qi, kj, lam, bias_far, qb, kb, vb, bias_patches, g_out)


def _new_token_mask(rows, t_new):
    row = lax.broadcasted_iota(jnp.int32, (rows, PAGE_SIZE), 0)
    col = lax.broadcasted_iota(jnp.int32, (rows, PAGE_SIZE), 1)
    return col <= (row % t_new)


def _softmax_pv(s_ref, v_ref, n_total):
    rows = s_ref.shape[1]
    unroll = PAGES_PER_STEP

    def page_max(n, m):
        return jnp.maximum(m, s_ref[n])

    m = lax.fori_loop(0, n_total, page_max, jnp.full((rows, PAGE_SIZE), NEG_INF, F32), unroll=unroll)
    m = jnp.max(m, axis=-1, keepdims=True)

    def page_pv(n, carry):
        l, acc = carry
        p = jnp.exp2(s_ref[n] - m)
        return l + p, acc + _dot(p.astype(BF16), v_ref[n])

    l, acc = lax.fori_loop(0, n_total, page_pv,
                           (jnp.zeros((rows, PAGE_SIZE), F32), jnp.zeros((rows, v_ref.shape[2]), F32)), unroll=unroll)
    return acc / jnp.sum(l, axis=-1, keepdims=True)


def _page_gather(pt_ref, hbm_a, hbm_b, buf_a, buf_b, sem, li, n_pages, n_seq, n_steps):
    ahead = PAGE_SLOTS - 1
    total = n_seq * n_steps
    assert total > ahead
    step = pl.program_id(0) * n_steps + pl.program_id(1)

    def copies(s):
        if isinstance(s, int):
            bb, jj, sl = s // n_steps, s % n_steps, s % PAGE_SLOTS
        else:
            bb, jj, sl = lax.div(s, n_steps), lax.rem(s, n_steps), lax.rem(s, PAGE_SLOTS)
        out = []
        for n in range(n_pages):
            page = pt_ref[bb, jj * n_pages + n]
            out.append((pltpu.make_async_copy(hbm_a.at[li, page], buf_a.at[sl, n], sem.at[0, sl]), n % 2))
            out.append((pltpu.make_async_copy(hbm_b.at[li, page], buf_b.at[sl, n], sem.at[1, sl]), n % 2))
        return out

    @pl.when(step == 0)
    def _():
        for s in range(ahead):
            for c, prio in copies(s):
                c.start(priority=prio)

    @pl.when(step + ahead < total)
    def _():
        for c, prio in copies(step + ahead):
            c.start(priority=prio)

    for c, _ in copies(step):
        c.wait()
    return lax.rem(step, PAGE_SLOTS)


def _mla_sample_kernel(pt_ref, ql_ref, qr_ref, nl_ref, nr_ref, wuv_ref, lat_hbm, rope_hbm, o_ref,
                       s_ref, v_ref, lat_buf, rope_buf, sem, *, n_pages, t_new, li, n_seq, n_steps):
    j = pl.program_id(1)
    ql, qr = ql_ref[0], qr_ref[0]
    n_past = n_pages * n_steps
    slot = _page_gather(pt_ref, lat_hbm, rope_hbm, lat_buf, rope_buf, sem, li, n_pages, n_seq, n_steps)
    for n in range(n_pages):
        lat = lat_buf[slot, n].astype(BF16)
        v_ref[j * n_pages + n] = lat
        s_ref[j * n_pages + n] = _dot_nt(ql, lat) + _dot(qr, rope_buf[slot, n].astype(BF16))

    @pl.when(j == pl.num_programs(1) - 1)
    def _():
        lat = nl_ref[0].astype(BF16)
        v_ref[n_past] = lat
        s_new = _dot_nt(ql, lat) + _dot_nt(qr, nr_ref[0].astype(BF16))
        s_ref[n_past] = jnp.where(_new_token_mask(ql.shape[0], t_new), s_new, NEG_INF)
        o_lat = _softmax_pv(s_ref, v_ref, n_past + 1)
        for h in range(H_A):
            o_ref[0, :, h * DV_A:(h + 1) * DV_A] = _dot(o_lat[h * t_new:(h + 1) * t_new].astype(BF16),
                                                      wuv_ref[h]).astype(o_ref.dtype)


def _mla_sample(page_table, q_lat, q_rope, new_lat, new_rope, w_uv, cache_lat, cache_rope_t, li):
    b, rows, _ = q_lat.shape
    t_new = rows // H_A
    n_pages = MLA_PAGES_PER_STEP
    total = page_table.shape[1]
    steps = total // n_pages
    per_seq = lambda r, w: pl.BlockSpec((1, r, w), lambda bi, j, pt: (bi, 0, 0))
    hbm = pl.BlockSpec(memory_space=pl.ANY)
    return pl.pallas_call(
        functools.partial(_mla_sample_kernel, n_pages=n_pages, t_new=t_new, li=li, n_seq=b, n_steps=steps),
        grid_spec=pltpu.PrefetchScalarGridSpec(
            num_scalar_prefetch=1,
            grid=(b, steps),
            in_specs=[per_seq(rows, KV_LORA), per_seq(rows, D_ROPE), per_seq(PAGE_SIZE, KV_LORA),
                      per_seq(PAGE_SIZE, D_ROPE),
                      pl.BlockSpec((H_A, KV_LORA, DV_A), lambda bi, j, pt: (0, 0, 0)), hbm, hbm],
            out_specs=pl.BlockSpec((1, t_new, BRANCH_W), lambda bi, j, pt: (bi, 0, 0)),
            scratch_shapes=[pltpu.VMEM((total + 1, rows, PAGE_SIZE), F32),
                            pltpu.VMEM((total + 1, PAGE_SIZE, KV_LORA), BF16),
                            pltpu.VMEM((PAGE_SLOTS, n_pages, PAGE_SIZE, KV_LORA), F32),
                            pltpu.VMEM((PAGE_SLOTS, n_pages, D_ROPE, PAGE_SIZE), F32),
                            pltpu.SemaphoreType.DMA((2, PAGE_SLOTS))]),
        out_shape=jax.ShapeDtypeStruct((b, t_new, BRANCH_W), F32),
        compiler_params=_params("arbitrary", "arbitrary"),
    )(page_table, q_lat, q_rope, new_lat, new_rope, w_uv, cache_lat, cache_rope_t)


def _diff_sample_kernel(pt_ref, lam_ref, q_ref, nk_ref, nv_ref, bias_ref, nbias_ref, g_ref, k_hbm, v_hbm, o_ref,
                        s_ref, vs_ref, k_buf, v_buf, sem, *, n_pages, t_new, out_scale, li, n_seq, n_steps):
    j = pl.program_id(1)
    q = q_ref[0]
    n_past = n_pages * n_steps
    slot = _page_gather(pt_ref, k_hbm, v_hbm, k_buf, v_buf, sem, li, n_pages, n_seq, n_steps)
    for n in range(n_pages):
        bias = bias_ref[:, n * PAGE_SIZE:(n + 1) * PAGE_SIZE]
        vs_ref[j * n_pages + n] = v_buf[slot, n].astype(BF16)
        s_ref[j * n_pages + n] = _dot_nt(q, k_buf[slot, n].astype(BF16)) + jnp.concatenate([bias, bias], axis=0)

    @pl.when(j == pl.num_programs(1) - 1)
    def _():
        nbias = nbias_ref[...]
        vs_ref[n_past] = nv_ref[0].astype(BF16)
        s_new = _dot_nt(q, nk_ref[0].astype(BF16)) + jnp.concatenate([nbias, nbias], axis=0)
        s_ref[n_past] = jnp.where(_new_token_mask(q.shape[0], t_new), s_new, NEG_INF)
        a = _softmax_pv(s_ref, vs_ref, n_past + 1)
        half = H_B * t_new
        o = _rms(a[:half] - lam_ref[0] * a[half:], g_ref[...]) * out_scale
        for h in range(H_B):
            o_ref[0, :, h * DV_B:(h + 1) * DV_B] = o[h * t_new:(h + 1) * t_new].astype(o_ref.dtype)


def _diff_sample(page_table, lam, q_pad, new_k, new_v, bias, new_bias, g_out, out_scale, cache_k, cache_v, li):
    b, rows, _ = q_pad.shape
    t_new = rows // (2 * H_B)
    n_pages = PAGES_PER_STEP
    total = page_table.shape[1]
    steps = total // n_pages
    per_seq = lambda r, w: pl.BlockSpec((1, r, w), lambda bi, j, pt, lm: (bi, 0, 0))
    hbm = pl.BlockSpec(memory_space=pl.ANY)
    return pl.pallas_call(
        functools.partial(_diff_sample_kernel, n_pages=n_pages, t_new=t_new, out_scale=out_scale, li=li, n_seq=b,
                          n_steps=steps),
        grid_spec=pltpu.PrefetchScalarGridSpec(
            num_scalar_prefetch=2,
            grid=(b, steps),
            in_specs=[per_seq(rows, 2 * DK_B), per_seq(PAGE_SIZE, 2 * DK_B), per_seq(PAGE_SIZE, DV_B),
                      pl.BlockSpec((rows // 2, n_pages * PAGE_SIZE), lambda bi, j, pt, lm: (0, j)),
                      pl.BlockSpec((rows // 2, PAGE_SIZE), lambda bi, j, pt, lm: (0, 0)),
                      pl.BlockSpec((1, DV_B), lambda bi, j, pt, lm: (0, 0)), hbm, hbm],
            out_specs=pl.BlockSpec((1, t_new, BRANCH_W), lambda bi, j, pt, lm: (bi, 0, 0)),
            scratch_shapes=[pltpu.VMEM((total + 1, rows, PAGE_SIZE), F32),
                            pltpu.VMEM((total + 1, PAGE_SIZE, DV_B), BF16),
                            pltpu.VMEM((PAGE_SLOTS, n_pages, PAGE_SIZE, 2 * DK_B), F32),
                            pltpu.VMEM((PAGE_SLOTS, n_pages, PAGE_SIZE, DV_B), F32),
                            pltpu.SemaphoreType.DMA((2, PAGE_SLOTS))]),
        out_shape=jax.ShapeDtypeStruct((b, t_new, BRANCH_W), F32),
        compiler_params=_params("arbitrary", "arbitrary"),
    )(page_table, lam, q_pad, new_k, new_v, bias, new_bias, g_out, cache_k, cache_v)


def _cumsum_rows(x):
    n = x.shape[0]
    row = lax.broadcasted_iota(jnp.int32, x.shape, 0)
    sh = 1
    while sh < n:
        x = x + jnp.where(row >= sh, pltpu.roll(x, sh, 0), 0.0)
        sh *= 2
    return x


def _group_row(x, group, j):
    n, w = x.shape
    if group == n:
        return jnp.broadcast_to(x[j:j + 1, :], (n, w))
    x3 = x.reshape(n // group, group, w)
    return jnp.broadcast_to(x3[:, j:j + 1, :], x3.shape).reshape(n, w)


def _hgrn_gates(cq, cf, a_lb, c_lb, oml):
    q = cq * jax.nn.sigmoid(cq)
    log_sig = jnp.minimum(cf, 0.0) - jnp.log1p(jnp.exp(-jnp.abs(cf)))
    x = c_lb + log_sig
    g = jnp.maximum(a_lb, x) + jnp.log1p(jnp.exp(-jnp.abs(a_lb - x)))
    k = oml * jax.nn.sigmoid(-cf)
    return q, k, g


def _hgrn_tile(q, k, v, g, st, sub):
    n = q.shape[0]
    b = _cumsum_rows(g)
    row = lax.broadcasted_iota(jnp.int32, (n, 1), 0)
    o = _dot_nt((q * jnp.exp(b)).astype(BF16), st.astype(BF16))
    for j in range(sub):
        e = jnp.where((row % sub) >= j, b - _group_row(b, sub, j), NEG_INF)
        r = jnp.sum(q * _group_row(k, sub, j) * jnp.exp(e), axis=-1, keepdims=True)
        o = o + r * _group_row(v, sub, j)
    half = sub
    while half < n:
        upper = (row % (2 * half)) >= half
        bb = _group_row(b, 2 * half, half - 1)
        qe = q * jnp.exp(jnp.where(upper, b - bb, NEG_INF))
        ke = k * jnp.exp(jnp.where(upper, NEG_INF, bb - b))
        s = _dot_nt(qe.astype(BF16), ke.astype(BF16))
        trow = lax.broadcasted_iota(jnp.int32, (n, n), 0) // (2 * half)
        tcol = lax.broadcasted_iota(jnp.int32, (n, n), 1) // (2 * half)
        o = o + _dot(jnp.where(trow == tcol, s, 0.0).astype(BF16), v.astype(BF16))
        half *= 2
    b_last = b[n - 1:n, :]
    kd = k * jnp.exp(b_last - b)
    st_next = st * jnp.exp(b_last) + _dot_tn(v.astype(BF16), kd.astype(BF16))
    return o, st_next


def _hgrn_prompt_kernel(cq_ref, cf_ref, ci_ref, cg_ref, a_ref, c_ref, oml_ref, go_ref, o_ref, s_ref, st_ref):
    t = pl.program_id(0)

    @pl.when(t == 0)
    def _():
        st_ref[...] = jnp.zeros(st_ref.shape, F32)

    for h in range(H_C):
        hl = slice(h * DK_C, (h + 1) * DK_C)
        q, k, g = _hgrn_gates(cq_ref[:, hl], cf_ref[:, hl], a_ref[:, hl], c_ref[:, hl], oml_ref[:, hl])
        o, st_next = _hgrn_tile(q, k, ci_ref[:, hl], g, st_ref[h], SUB)
        st_ref[h] = st_next
        o_ref[:, hl] = (_rms(o, go_ref[...]) * jax.nn.sigmoid(cg_ref[:, hl])).astype(o_ref.dtype)

        @pl.when(t == pl.num_programs(0) - 1)
        def _():
            s_ref[h] = st_next.T


def _hgrn_prompt(z_c, a_lb, c_lb, oml, g_out, t_len):
    tt = T_REC
    w = H_C * DK_C
    sect = lambda s: pl.BlockSpec((tt, w), functools.partial(lambda t, s: (t, s), s=s))
    vec = pl.BlockSpec((1, w), lambda t: (0, 0))
    return pl.pallas_call(
        _hgrn_prompt_kernel,
        grid=(t_len // tt,),
        in_specs=[sect(0), sect(1), sect(2), sect(3), vec, vec, vec, pl.BlockSpec((1, DV_C), lambda t: (0, 0))],
        out_specs=[pl.BlockSpec((tt, BRANCH_W), lambda t: (t, 0)),
                   pl.BlockSpec((H_C, DK_C, DV_C), lambda t: (0, 0, 0))],
        out_shape=[jax.ShapeDtypeStruct((t_len, BRANCH_W), BF16),
                   jax.ShapeDtypeStruct((H_C, DK_C, DV_C), F32)],
        scratch_shapes=[pltpu.VMEM((H_C, DV_C, DK_C), F32)],
        compiler_params=_params("arbitrary"),
    )(z_c, z_c, z_c, z_c, a_lb, c_lb, oml, g_out)


def _pad_rows(x, n):
    return jnp.concatenate([x, jnp.zeros((n - x.shape[0], x.shape[1]), x.dtype)], axis=0)


def _hgrn_sample_kernel(z_ref, s0_ref, a_ref, c_ref, oml_ref, go_ref, o_ref, s_ref):
    n = z_ref.shape[1]
    w = H_C * DK_C
    row = lax.broadcasted_iota(jnp.int32, (n, 1), 0)
    for sq in range(z_ref.shape[0]):
        z = z_ref[sq]
        for h in range(H_C):
            lanes = lambda s: slice(s * w + h * DK_C, s * w + (h + 1) * DK_C)
            hl = slice(h * DK_C, (h + 1) * DK_C)
            q, k, g = _hgrn_gates(z[:, lanes(0)], z[:, lanes(1)], a_ref[:, hl], c_ref[:, hl], oml_ref[:, hl])
            v = z[:, lanes(2)]
            s0 = s0_ref[0, sq, h]
            b = _cumsum_rows(g)
            o = _dot((q * jnp.exp(b)).astype(BF16), s0.astype(BF16))
            for j in range(n):
                e = jnp.where(row >= j, b - b[j:j + 1, :], NEG_INF)
                r = jnp.sum(q * k[j:j + 1, :] * jnp.exp(e), axis=-1, keepdims=True)
                o = o + r * v[j:j + 1, :]
            b_last = b[n - 1:n, :]
            kd = _pad_rows((k * jnp.exp(b_last - b)).astype(BF16), LANES)
            st_next = s0.T * jnp.exp(b_last) + _dot_tn(_pad_rows(v.astype(BF16), LANES), kd)
            s_ref[sq, h] = st_next.T
            o_ref[sq, :, hl] = (_rms(o, go_ref[...]) * jax.nn.sigmoid(z[:, lanes(3)])).astype(o_ref.dtype)


def _hgrn_sample(z_c, state, li, a_lb, c_lb, oml, g_out):
    b, t_new, w = z_c.shape
    ns = SEQS_PER_STEP
    assert b % ns == 0
    vec = pl.BlockSpec((1, H_C * DK_C), lambda i: (0, 0))
    return pl.pallas_call(
        _hgrn_sample_kernel,
        grid=(b // ns,),
        in_specs=[pl.BlockSpec((ns, t_new, w), lambda i: (i, 0, 0)),
                  pl.BlockSpec((1, ns, H_C, DK_C, DV_C), lambda i: (li, i, 0, 0, 0)),
                  vec, vec, vec, pl.BlockSpec((1, DV_C), lambda i: (0, 0))],
        out_specs=[pl.BlockSpec((ns, t_new, BRANCH_W), lambda i: (i, 0, 0)),
                   pl.BlockSpec((ns, H_C, DK_C, DV_C), lambda i: (i, 0, 0, 0))],
        out_shape=[jax.ShapeDtypeStruct((b, t_new, BRANCH_W), F32),
                   jax.ShapeDtypeStruct((b, H_C, DK_C, DV_C), F32)],
        compiler_params=_params("parallel"),
    )(z_c, state, a_lb, c_lb, oml, g_out)


def _ret_log_decay():
    return np.log1p(-np.exp2(-5.0 - np.arange(H_D, dtype=np.float64)))


def _ret_tables(n, block):
    lg = _ret_log_decay()
    t = np.arange(n)
    loc = t % block
    same = (t[:, None] // block) == (t[None, :] // block)
    diff = loc[:, None] - loc[None, :]
    dmat = np.where(same & (diff >= 0), np.exp(lg[:, None, None] * np.maximum(diff, 0)[None]), 0.0)
    per_lane = lambda f: np.repeat(f, DK_D, axis=1)
    dq = per_lane(np.exp(lg[None, :] * (loc[:, None] + 1)))
    dk = per_lane(np.exp(lg[None, :] * (block - 1 - loc[:, None])))
    dend = per_lane(np.exp(lg[None, :] * block))
    return tuple(jnp.asarray(a, F32) for a in (dmat, dq, dk, dend))


def _ret_prompt_kernel(rq_ref, rk_ref, rv_ref, rg_ref, cos_ref, sin_ref, dmat_ref, dq_ref, dk_ref, dend_ref, go_ref,
                       o_ref, s_ref, st_ref):
    t = pl.program_id(0)

    @pl.when(t == 0)
    def _():
        st_ref[...] = jnp.zeros(st_ref.shape, F32)

    cos, sin = cos_ref[...], sin_ref[...]
    q = _rope64(rq_ref[...], cos, sin)
    k = _rope64(rk_ref[...], cos, sin) * (DK_D ** -0.5)
    lane = lax.broadcasted_iota(jnp.int32, q.shape, 1)
    k_bf = k.astype(BF16)
    qd = q * dq_ref[...]
    kd = k * dk_ref[...]
    for h in range(H_D):
        in_head = (lane // DK_D) == h
        vl = slice(h * DV_D, (h + 1) * DV_D)
        v = rv_ref[:, vl].astype(BF16)
        st = st_ref[h]
        s = _dot_nt(jnp.where(in_head, q, 0.0).astype(BF16), k_bf) * dmat_ref[h]
        o = _dot(s.astype(BF16), v) + _dot_nt(jnp.where(in_head, qd, 0.0).astype(BF16), st.astype(BF16))
        st_next = st * dend_ref[...] + _dot_tn(v, jnp.where(in_head, kd, 0.0).astype(BF16))
        st_ref[h] = st_next
        o_ref[:, vl] = (_rms(o, go_ref[...]) * (rg_ref[:, vl] * jax.nn.sigmoid(rg_ref[:, vl]))).astype(o_ref.dtype)

        @pl.when(t == pl.num_programs(0) - 1)
        def _():
            s_ref[h] = st_next.T[h * DK_D:(h + 1) * DK_D, :]


def _ret_prompt(z_d, cos, sin, g_out, t_len):
    tt = T_RET
    wq = H_D * DK_D
    wv = H_D * DV_D
    dmat, dq, dk, dend = _ret_tables(tt, tt)
    const = lambda *shape: pl.BlockSpec(shape, lambda t: (0,) * len(shape))
    return pl.pallas_call(
        _ret_prompt_kernel,
        grid=(t_len // tt,),
        in_specs=[pl.BlockSpec((tt, wq), lambda t: (t, 0)), pl.BlockSpec((tt, wq), lambda t: (t, 1)),
                  pl.BlockSpec((tt, wv), lambda t: (t, 1)), pl.BlockSpec((tt, wv), lambda t: (t, 2)),
                  pl.BlockSpec((tt, wq), lambda t: (t, 0)), pl.BlockSpec((tt, wq), lambda t: (t, 0)),
                  const(H_D, tt, tt), const(tt, wq), const(tt, wq), const(1, wq), const(1, DV_D)],
        out_specs=[pl.BlockSpec((tt, BRANCH_W), lambda t: (t, 0)),
                   pl.BlockSpec((H_D, DK_D, DV_D), lambda t: (0, 0, 0))],
        out_shape=[jax.ShapeDtypeStruct((t_len, BRANCH_W), BF16),
                   jax.ShapeDtypeStruct((H_D, DK_D, DV_D), F32)],
        scratch_shapes=[pltpu.VMEM((H_D, DV_D, wq), F32)],
        compiler_params=_params("arbitrary"),
    )(z_d, z_d, z_d, z_d, cos, sin, dmat, dq, dk, dend, g_out)


def _ret_sample_kernel(z_ref, s0_ref, cos_ref, sin_ref, dmat_ref, dq_ref, dk_ref, dend_ref, go_ref, o_ref, s_ref):
    wq = H_D * DK_D
    cos, sin = cos_ref[...], sin_ref[...]
    for sq in range(z_ref.shape[0]):
        z = z_ref[sq]
        q = _rope64(z[:, :wq], cos, sin)
        k = _rope64(z[:, wq:2 * wq], cos, sin) * (DK_D ** -0.5)
        lane = lax.broadcasted_iota(jnp.int32, q.shape, 1)
        k_pad = _pad_rows(k.astype(BF16), LANES)
        qd = q * dq_ref[...]
        kd = _pad_rows((k * dk_ref[...]).astype(BF16), LANES)
        lane_pad = lax.broadcasted_iota(jnp.int32, kd.shape, 1)
        s_all = s0_ref[0, sq].reshape(wq, DV_D)
        for h in range(H_D):
            in_head = (lane // DK_D) == h
            rows = slice(h * DK_D, (h + 1) * DK_D)
            vl = slice(2 * wq + h * DV_D, 2 * wq + (h + 1) * DV_D)
            gl = slice(2 * wq + H_D * DV_D + h * DV_D, 2 * wq + H_D * DV_D + (h + 1) * DV_D)
            v_pad = _pad_rows(z[:, vl].astype(BF16), LANES)
            s = _dot_nt(jnp.where(in_head, q, 0.0).astype(BF16), k_pad) * dmat_ref[h]
            o = _dot(s.astype(BF16), v_pad) + _dot(jnp.where(in_head, qd, 0.0).astype(BF16), s_all.astype(BF16))
            upd = _dot_tn(jnp.where((lane_pad // DK_D) == h, kd, jnp.zeros_like(kd)), v_pad)
            s_ref[sq, h] = s_all[rows] * dend_ref[rows] + upd[rows]
            o_ref[sq, :, h * DV_D:(h + 1) * DV_D] = (
                _rms(o, go_ref[...]) * (z[:, gl] * jax.nn.sigmoid(z[:, gl]))).astype(o_ref.dtype)


def _ret_sample(z_d, state, li, cos, sin, g_out):
    b, t_new, w = z_d.shape
    wq = H_D * DK_D
    dmat, dq, dk, dend = _ret_tables(t_new, t_new)
    dmat = jnp.pad(dmat, ((0, 0), (0, 0), (0, LANES - t_new)))
    const = lambda *shape: pl.BlockSpec(shape, lambda i: (0,) * len(shape))
    ns = SEQS_PER_STEP
    assert b % ns == 0
    return pl.pallas_call(
        _ret_sample_kernel,
        grid=(b // ns,),
        in_specs=[pl.BlockSpec((ns, t_new, w), lambda i: (i, 0, 0)),
                  pl.BlockSpec((1, ns, H_D, DK_D, DV_D), lambda i: (li, i, 0, 0, 0)),
                  const(t_new, wq), const(t_new, wq), const(H_D, t_new, LANES), const(t_new, wq), const(t_new, wq),
                  const(wq, 1), const(1, DV_D)],
        out_specs=[pl.BlockSpec((ns, t_new, BRANCH_W), lambda i: (i, 0, 0)),
                   pl.BlockSpec((ns, H_D, DK_D, DV_D), lambda i: (i, 0, 0, 0))],
        out_shape=[jax.ShapeDtypeStruct((b, t_new, BRANCH_W), F32),
                   jax.ShapeDtypeStruct((b, H_D, DK_D, DV_D), F32)],
        compiler_params=_params("parallel"),
    )(z_d, state, cos, sin, dmat, dq, dk, dend.reshape(wq, 1), g_out)


def _t5_bucket(rel):
    n = jnp.maximum(jnp.asarray(rel, jnp.int32), 0)
    large = T5_MAX_EXACT + (jnp.log(jnp.maximum(n, 1).astype(F32) / T5_MAX_EXACT)
                            / math.log(T5_MAX_DIST / T5_MAX_EXACT)
                            * (N_BUCKETS - T5_MAX_EXACT)).astype(jnp.int32)
    return jnp.where(n < T5_MAX_EXACT, n, jnp.minimum(large, N_BUCKETS - 1))


def _bias_lookup(table, buckets):
    out = jnp.zeros((table.shape[1],) + buckets.shape, F32)
    expand = (slice(None),) + (None,) * buckets.ndim
    for b in range(N_BUCKETS):
        out = jnp.where(buckets[None] == b, table[b][expand], out)
    return out


def _layer_weights(li, w_in, mla_w_qb, mla_w_uk, mla_w_uv, w_branch, w_out, w_gate, w_up, w_down):
    offs = np.cumsum([0, Q_LORA, KV_LORA, D_ROPE, H_B * 2 * DK_B, 2 * DK_B, DV_B, H_C * DK_C, H_C * DK_C, H_C * DV_C,
                      H_C * DV_C, H_D * DK_D, H_D * DK_D, H_D * DV_D, H_D * DV_D, N_BRANCH * D_MODEL])
    w = w_in[li]
    col = lambda a, b: w[:, offs[a]:offs[b]]
    w_a = jnp.concatenate([col(0, 2)] + [col(2, 3)] * H_A, axis=1)
    groups = [w_a, col(3, 6), col(6, 10), col(10, 14), col(14, 15)]
    wqb = mla_w_qb[li]
    return {
        "w_in": [g.astype(BF16) for g in groups],
        "w_nope": jnp.transpose(wqb[:, :, :D_NOPE], (1, 0, 2)).astype(BF16),
        "w_rope": wqb[:, :, D_NOPE:].reshape(Q_LORA, H_A * D_ROPE).astype(BF16),
        "w_ukT": jnp.transpose(mla_w_uk[li], (1, 2, 0)).astype(BF16),
        "w_uv": jnp.transpose(mla_w_uv[li], (1, 0, 2)).astype(BF16),
        "w_branch": w_branch[li].astype(BF16),
        "w_out": w_out[li].astype(BF16),
        "w_gate": w_gate[li].astype(BF16),
        "w_up": w_up[li].astype(BF16),
        "w_down": w_down[li].astype(BF16),
    }


def kernel(x_prompt, x_sample, cache_mla_latent, cache_mla_rope, cache_diff_k, cache_diff_v, state_hgrn, state_ret, page_table, t5_bias, norm_attn, w_in, mla_q_norm, mla_w_qb, mla_q_head_norm, mla_kv_norm, mla_krope_norm, mla_w_uk, mla_w_uv, diff_q_norm, diff_k_norm, diff_lambda, diff_out_norm, hgrn_lb_logits, hgrn_out_norm, ret_out_norm, w_branch, w_out, norm_ffn, w_gate, w_up, w_down):
    n_batch, t_p, _ = x_prompt.shape
    n_seq, t_s, _ = x_sample.shape
    assert n_batch == 1 and t_p % T_ATTN == 0
    past = page_table.shape[1] * PAGE_SIZE
    n_s = n_seq * t_s
    assert (t_p + n_s) % TM_DENSE == 0
    assert page_table.shape[1] % PAGES_PER_STEP == 0 and page_table.shape[1] % MLA_PAGES_PER_STEP == 0

    x = jnp.concatenate([x_prompt.reshape(t_p, D_MODEL), x_sample.reshape(n_s, D_MODEL)], axis=0)
    pos = jnp.concatenate([jnp.arange(t_p, dtype=jnp.int32), jnp.tile(past + jnp.arange(t_s, dtype=jnp.int32), n_seq)])
    cos, sin = _rope_tables(pos, H_A * D_ROPE)
    cos_s, sin_s = cos[t_p:t_p + t_s], sin[t_p:t_p + t_s]

    ta = np.arange(T_ATTN)
    rel = ta[:, None] - ta[None, :]
    bias_table = t5_bias.astype(F32) * LOG2E
    patch_buckets = jnp.stack([_t5_bucket(rel + T_ATTN), _t5_bucket(rel)])
    bias_patches = jnp.transpose(_bias_lookup(bias_table, patch_buckets), (1, 0, 2, 3))
    bias_far = _bias_lookup(bias_table, _t5_bucket(np.full((1,), 2 * T_ATTN))).reshape(H_B)
    q_pos = past + np.arange(t_s)
    strip_buckets = _t5_bucket(q_pos[:, None] - np.arange(past + PAGE_SIZE)[None, :])
    bias_strip = _bias_lookup(bias_table, strip_buckets).reshape(H_B * t_s, past + PAGE_SIZE)
    bias_past, bias_new = bias_strip[:, :past], bias_strip[:, past:]

    lb_cum = jnp.cumsum(jax.nn.softmax(hgrn_lb_logits.astype(F32), axis=0), axis=0)
    lb_all = lb_cum - lb_cum[:1]

    cache_rope_t = jnp.swapaxes(cache_mla_rope, 2, 3)
    cache_k = cache_diff_k.reshape(cache_diff_k.shape[:3] + (2 * DK_B,))
    cache_v = cache_diff_v.reshape(cache_diff_v.shape[:3] + (DV_B,))
    pad_new = lambda a: jnp.pad(a.reshape(n_seq, t_s, a.shape[-1]), ((0, 0), (0, PAGE_SIZE - t_s), (0, 0)))

    outs_p, outs_s = [], []
    for li in range(DEPTH):
        wts = _layer_weights(li, w_in, mla_w_qb, mla_w_uk, mla_w_uv, w_branch, w_out, w_gate, w_up, w_down)
        z_a, z_b, z_c, z_d, z_g = [
            _norm_matmul(x, norm_attn[li], w, tn, dt)
            for w, tn, dt in zip(wts["w_in"],
                                 (wts["w_in"][0].shape[1], wts["w_in"][1].shape[1], 2 * TN_DENSE,
                                  wts["w_in"][3].shape[1] // 2, 2 * TN_DENSE),
                                 (F32, F32, F32, F32, BF16))]

        pa = {"g_q": mla_q_norm[li].reshape(1, -1), "w_nope": wts["w_nope"], "w_rope": wts["w_rope"],
              "g_hn": mla_q_head_norm[li, :D_NOPE].reshape(1, -1),
              "g_hr": jnp.tile(mla_q_head_norm[li, D_NOPE:], H_A).reshape(1, -1), "w_ukT": wts["w_ukT"],
              "g_kv": mla_kv_norm[li].reshape(1, -1), "g_kr": jnp.tile(mla_krope_norm[li], H_A).reshape(1, -1)}
        qcat, kcat, c_kv, k_rope = _mla_prep(z_a, pa, cos, sin)
        o_a_p = _mla_prompt(qcat, kcat, wts["w_uv"], t_p)
        qs = qcat[:, t_p:].reshape(H_A, n_seq, t_s, -1)
        q_lat_s = jnp.transpose(qs[..., :KV_LORA], (1, 0, 2, 3)).reshape(n_seq, H_A * t_s, KV_LORA)
        q_rope_s = jnp.stack([qs[h, :, :, KV_LORA + h * D_ROPE:KV_LORA + (h + 1) * D_ROPE] for h in range(H_A)], axis=1)
        q_rope_s = q_rope_s.reshape(n_seq, H_A * t_s, D_ROPE)
        o_a_s = _mla_sample(page_table, q_lat_s, q_rope_s, pad_new(c_kv[t_p:]), pad_new(k_rope[t_p:]), wts["w_uv"],
                            cache_mla_latent, cache_rope_t, li)

        lam_init = 0.8 - 0.6 * math.exp(-0.3 * li)
        lv = diff_lambda[li].astype(F32)
        lam = (jnp.exp(jnp.sum(lv[0] * lv[1])) - jnp.exp(jnp.sum(lv[2] * lv[3])) + lam_init).reshape(1)
        qb, kb, vb = _diff_prep(z_b, jnp.tile(diff_q_norm[li], 2 * H_B).reshape(1, -1),
                                jnp.tile(diff_k_norm[li], 2).reshape(1, -1))
        g_diff = diff_out_norm[li].reshape(1, -1)
        o_b_p = _diff_prompt(qb, kb.astype(BF16), vb.astype(BF16), bias_patches, bias_far, lam, g_diff,
                             1.0 - lam_init, t_p)
        qbs = qb[t_p:].reshape(n_seq, t_s, H_B, 2, DK_B)
        zero = jnp.zeros_like(qbs[..., 0, :])
        q_pad = jnp.stack([jnp.concatenate([qbs[..., 0, :], zero], axis=-1),
                           jnp.concatenate([zero, qbs[..., 1, :]], axis=-1)], axis=1)
        q_pad = jnp.transpose(q_pad, (0, 1, 3, 2, 4)).reshape(n_seq, 2 * H_B * t_s, 2 * DK_B)
        o_b_s = _diff_sample(page_table, lam, q_pad, pad_new(kb[t_p:]), pad_new(vb[t_p:]), bias_past, bias_new, g_diff,
                             1.0 - lam_init, cache_k, cache_v, li)

        lb = lb_all[li].reshape(1, -1)
        a_lb, c_lb, oml = jnp.log(lb), jnp.log1p(-lb), 1.0 - lb
        g_hgrn = hgrn_out_norm[li].reshape(1, -1)
        o_c_p, s_hgrn_p = _hgrn_prompt(z_c, a_lb, c_lb, oml, g_hgrn, t_p)
        o_c_s, s_hgrn_s = _hgrn_sample(z_c[t_p:].reshape(n_seq, t_s, -1), state_hgrn, li, a_lb, c_lb, oml, g_hgrn)

        g_ret = ret_out_norm[li].reshape(1, -1)
        o_d_p, s_ret_p = _ret_prompt(z_d, cos, sin, g_ret, t_p)
        o_d_s, s_ret_s = _ret_sample(z_d[t_p:].reshape(n_seq, t_s, -1), state_ret, li, cos_s, sin_s, g_ret)

        mixed = _merge([o_a_p, o_b_p, o_c_p, o_d_p],
                       [o.reshape(n_s, BRANCH_W) for o in (o_a_s, o_b_s, o_c_s, o_d_s)], wts["w_branch"], z_g)
        x = _matmul_res(mixed, wts["w_out"], x, TM_DENSE, 2 * TN_DENSE)
        u = _ffn_up(x, norm_ffn[li], wts["w_gate"], wts["w_up"])
        x = _matmul_res(u, wts["w_down"], x, TM_DENSE, TN_DENSE)

        outs_p.append((c_kv[:t_p].reshape(1, t_p, KV_LORA), k_rope[:t_p].reshape(1, t_p, D_ROPE),
                       kb[:t_p].reshape(1, t_p, 1, 2 * DK_B), vb[:t_p].reshape(1, t_p, 1, DV_B),
                       s_hgrn_p[None], s_ret_p[None]))
        outs_s.append((c_kv[t_p:].reshape(n_seq, t_s, KV_LORA), k_rope[t_p:].reshape(n_seq, t_s, D_ROPE),
                       kb[t_p:].reshape(n_seq, t_s, 1, 2 * DK_B), vb[t_p:].reshape(n_seq, t_s, 1, DV_B),
                       s_hgrn_s, s_ret_s))

    stack = lambda outs: [jnp.stack(a) for a in zip(*outs)]
    y_p = x[:t_p].reshape(1, t_p, D_MODEL)
    y_s = x[t_p:].reshape(n_seq, t_s, D_MODEL)
    return (y_p, y_s, *stack(outs_p), *stack(outs_s))
```
